```python
import math
import jax, jax.numpy as jnp
from jax import lax
import numpy as np

D_MODEL = 1024
BATCH = 32
SEQ = 2048
DEPTH = 4
DEC_BATCH = 8
DEC_SEQ = 64
PAST_LEN = 2048

CHUNK = 64
Q_BLOCK = 128
PLE_DIM = 256
EPS = 1e-6
A_HEADS = 8
A_DH = 64
A_DV = 2 * A_DH
A_QK = A_HEADS * 2 * A_DH
A_WIDTH = A_HEADS * A_DV
ROPE_DIM = A_DH // 4
ROPE_THETA = 500000.0
G_HEADS = 4
G_DK = D_MODEL // 2 // G_HEADS
G_DV = D_MODEL // G_HEADS
G_KW = G_HEADS * G_DK
G_VW = G_HEADS * G_DV
G_RANK = 16
G_NORMALIZER = 16.0
SUB = 16
D_FF = ((8 * D_MODEL // 3 + 255) // 256) * 256
IN_SPLITS = (A_QK, A_QK, A_WIDTH, G_KW, G_KW, G_VW, G_VW, G_RANK, D_MODEL, D_MODEL)
IN_WIDTH = A_QK + A_QK + A_WIDTH + G_KW + G_KW + G_VW + G_VW + G_RANK + D_MODEL + D_MODEL

kernel_name = 'chunk_stream_diffattn_gla_hybrid'


def rmsnorm(x, w):
    xf = x.astype(jnp.float32)
    y = xf * lax.rsqrt(jnp.mean(xf * xf, axis=-1, keepdims=True) + EPS)
    return (y * w.astype(jnp.float32)).astype(x.dtype)


def rope_partial(x, pos):
    half = ROPE_DIM // 2
    inv = ROPE_THETA ** (-jnp.arange(0, ROPE_DIM, 2, dtype=jnp.float32) / ROPE_DIM)
    ang = pos.astype(jnp.float32)[:, None] * inv[None, :]
    cos = jnp.cos(ang)[None, :, None, None, :]
    sin = jnp.sin(ang)[None, :, None, None, :]
    xr = x[..., :ROPE_DIM].astype(jnp.float32)
    x1, x2 = xr[..., :half], xr[..., half:]
    rot = jnp.concatenate([x1 * cos - x2 * sin, x2 * cos + x1 * sin], axis=-1)
    return jnp.concatenate([rot.astype(x.dtype), x[..., ROPE_DIM:]], axis=-1)


def diff_attend(q, k, v, lam, mask):
    s = jnp.einsum('bqhmd,bkhmd->bhmqk', q.astype(jnp.float32), k.astype(jnp.float32)) * (A_DH ** -0.5)
    if mask is not None:
        s = jnp.where(mask, s, -jnp.inf)
    p = jax.nn.softmax(s, axis=-1)
    a = p[:, :, 0] - lam * p[:, :, 1]
    return jnp.einsum('bhqk,bkhe->bqhe', a, v.astype(jnp.float32)).astype(v.dtype)


def diff_attend_prompt(q, k, v, lam):
    B, T = q.shape[0], q.shape[1]
    nqb = T // Q_BLOCK
    k_chunk = jnp.arange(T) // CHUNK
    qb = jnp.moveaxis(q.reshape(B, nqb, Q_BLOCK, A_HEADS, 2, A_DH), 1, 0)

    def one(args):
        q_blk, i = args
        q_chunk = (i * Q_BLOCK + jnp.arange(Q_BLOCK)) // CHUNK
        mask = k_chunk[None, :] <= q_chunk[:, None]
        return diff_attend(q_blk, k, v, lam, mask)

    o = lax.map(one, (qb, jnp.arange(nqb)))
    return jnp.moveaxis(o, 0, 1).reshape(B, T, A_HEADS, A_DV)


def gla_chunk(S, q, k, v, g):
    B, H, C, DK = q.shape
    DV = v.shape[-1]
    n = C // SUB
    b = jnp.cumsum(g, axis=2)
    o_state = jnp.einsum('bhcd,bhde->bhce', q * jnp.exp(b), S)
    bs = b.reshape(B, H, n, SUB, DK)
    qs = q.reshape(B, H, n, SUB, DK)
    ks = k.reshape(B, H, n, SUB, DK)
    vs = v.reshape(B, H, n, SUB, DV)
    r = jnp.concatenate([jnp.zeros((B, H, 1, DK), b.dtype), bs[:, :, :-1, -1]], axis=2)
    qf = qs * jnp.exp(bs - r[:, :, :, None, :])
    before = jnp.arange(C)[None, :] < (jnp.arange(n) * SUB)[:, None]
    ek = jnp.where(before[None, None, :, :, None], r[:, :, :, None, :] - b[:, :, None, :, :], -jnp.inf)
    kf = k[:, :, None] * jnp.exp(ek)
    a_inter = jnp.einsum('bhnid,bhncd->bhnic', qf, kf)
    o_inter = jnp.einsum('bhnic,bhce->bhnie', a_inter, v)
    tri = jnp.arange(SUB)[:, None] >= jnp.arange(SUB)[None, :]
    ed = jnp.where(tri[..., None], bs[:, :, :, :, None, :] - bs[:, :, :, None, :, :], -jnp.inf)
    a_intra = jnp.einsum('bhnid,bhnijd,bhnjd->bhnij', qs, jnp.exp(ed), ks)
    o_intra = jnp.einsum('bhnij,bhnje->bhnie', a_intra, vs)
    o = o_state + (o_inter + o_intra).reshape(B, H, C, DV)
    b_last = b[:, :, -1]
    S_new = jnp.exp(b_last)[..., None] * S + jnp.einsum('bhcd,bhce->bhde', k * jnp.exp(b_last[:, :, None] - b), v)
    return S_new, o


def gla_prompt(q, k, v, g):
    B, T = q.shape[0], q.shape[1]
    nc = T // CHUNK

    def to_chunks(a):
        a = a.astype(jnp.float32).transpose(0, 2, 1, 3)
        return jnp.moveaxis(a.reshape(B, G_HEADS, nc, CHUNK, a.shape[-1]), 2, 0)

    S0 = jnp.zeros((B, G_HEADS, G_DK, G_DV), jnp.float32)
    S, o = lax.scan(lambda S_c, xs: gla_chunk(S_c, xs[0], xs[1], xs[2], xs[3]), S0,
                    (to_chunks(q), to_chunks(k), to_chunks(v), to_chunks(g)))
    o = jnp.moveaxis(o, 0, 2).reshape(B, G_HEADS, T, G_DV).transpose(0, 2, 1, 3)
    return o, S


def gla_sample(q, k, v, g, S0):
    T = q.shape[1]
    pad = CHUNK - T

    def prep(a):
        a = jnp.pad(a.astype(jnp.float32), ((0, 0), (0, pad), (0, 0), (0, 0)))
        return a.transpose(0, 2, 1, 3)

    S, o = gla_chunk(S0.astype(jnp.float32), prep(q), prep(k), prep(v), prep(g))
    return o[:, :, :T].transpose(0, 2, 1, 3), S


def layer(x, pe, pos, l, attend, recur, lw):
    (w_in, norm_mix, lq1, lk1, lq2, lk2, a_subln, w_pa, w_ga2, b_ga, g_norm, w_pb, w_o,
     norm_ffn, w_f1, w_f3, w_f2, norm_ple, w_ple, w_ple_gate) = lw
    B, T = x.shape[0], x.shape[1]
    h = rmsnorm(x, norm_mix)
    z = h @ w_in
    idx = []
    acc = 0
    for s in IN_SPLITS[:-1]:
        acc += s
        idx.append(acc)
    qa, ka, va, qg, kg, vg, rg, ag, ga, gb = jnp.split(z, idx, axis=-1)
    qa = rope_partial(qa.reshape(B, T, A_HEADS, 2, A_DH), pos)
    ka = rope_partial(ka.reshape(B, T, A_HEADS, 2, A_DH), pos)
    va = va.reshape(B, T, A_HEADS, A_DV)
    lam_init = 0.8 - 0.6 * math.exp(-0.3 * l)
    f32 = jnp.float32
    lam = (jnp.exp(jnp.sum(lq1.astype(f32) * lk1.astype(f32)))
           - jnp.exp(jnp.sum(lq2.astype(f32) * lk2.astype(f32))) + lam_init)
    oa = attend(qa, ka, va, lam)
    oa = rmsnorm(oa, a_subln) * (1.0 - lam_init)
    ua = oa.reshape(B, T, A_WIDTH) @ w_pa
    g = jax.nn.log_sigmoid((ag @ w_ga2 + b_ga).astype(f32)) / G_NORMALIZER
    og, S = recur(qg.reshape(B, T, G_HEADS, G_DK) * (G_DK ** -0.5),
                  kg.reshape(B, T, G_HEADS, G_DK),
                  vg.reshape(B, T, G_HEADS, G_DV),
                  g.reshape(B, T, G_HEADS, G_DK))
    og = rmsnorm(og, g_norm).reshape(B, T, G_VW) * jax.nn.silu(rg)
    ub = og.astype(x.dtype) @ w_pb
    mixed = jax.nn.sigmoid(ga) * ua + jax.nn.sigmoid(gb) * ub
    x = x + (mixed @ w_o).astype(x.dtype)
    h = rmsnorm(x, norm_ffn)
    x = x + ((jax.nn.silu(h @ w_f1) * (h @ w_f3)) @ w_f2).astype(x.dtype)
    h = rmsnorm(x, norm_ple)
    x = x + (jax.nn.sigmoid(h @ w_ple_gate) * (pe @ w_ple)).astype(x.dtype)
    return x, ka.reshape(B, T, A_HEADS, A_DV), va, S


def setup_inputs(seed: int = 0) -> dict:
    key = jax.random.key(seed)
    ks = jax.random.split(key, 32)
    f32 = jnp.float32

    def nrm(k, shape, scale):
        return jax.random.normal(k, shape, f32) * scale

    def gain(k, shape):
        return 1.0 + 0.01 * jax.random.normal(k, shape, f32)

    return {
        'x_prompt': nrm(ks[0], (BATCH, SEQ, D_MODEL), 1.0),
        'x_sample': nrm(ks[1], (DEC_BATCH, DEC_SEQ, D_MODEL), 1.0),
        'p_prompt': nrm(ks[2], (DEPTH, BATCH, SEQ, PLE_DIM), 1.0),
        'p_sample': nrm(ks[3], (DEPTH, DEC_BATCH, DEC_SEQ, PLE_DIM), 1.0),
        'cache_k': nrm(ks[4], (DEPTH, DEC_BATCH, PAST_LEN, A_HEADS, A_DV), 1.0),
        'cache_v': nrm(ks[5], (DEPTH, DEC_BATCH, PAST_LEN, A_HEADS, A_DV), 1.0),
        'state_gla': nrm(ks[6], (DEPTH, DEC_BATCH, G_HEADS, G_DK, G_DV), 0.5),
        'w_in': nrm(ks[7], (DEPTH, D_MODEL, IN_WIDTH), D_MODEL ** -0.5),
        'norm_mix': gain(ks[8], (DEPTH, D_MODEL)),
        'lambda_q1': nrm(ks[9], (DEPTH, A_DH), 0.1),
        'lambda_k1': nrm(ks[10], (DEPTH, A_DH), 0.1),
        'lambda_q2': nrm(ks[11], (DEPTH, A_DH), 0.1),
        'lambda_k2': nrm(ks[12], (DEPTH, A_DH), 0.1),
        'a_subln': gain(ks[13], (DEPTH, A_DV)),
        'w_pa': nrm(ks[14], (DEPTH, A_WIDTH, D_MODEL), A_WIDTH ** -0.5),
        'w_ga2': nrm(ks[15], (DEPTH, G_RANK, G_KW), G_RANK ** -0.5),
        'b_ga': nrm(ks[16], (DEPTH, G_KW), 0.1),
        'g_norm': gain(ks[17], (DEPTH, G_DV)),
        'w_pb': nrm(ks[18], (DEPTH, G_VW, D_MODEL), G_VW ** -0.5),
        'w_o': nrm(ks[19], (DEPTH, D_MODEL, D_MODEL), D_MODEL ** -0.5),
        'norm_ffn': gain(ks[20], (DEPTH, D_MODEL)),
        'w_f1': nrm(ks[21], (DEPTH, D_MODEL, D_FF), D_MODEL ** -0.5),
        'w_f3': nrm(ks[22], (DEPTH, D_MODEL, D_FF), D_MODEL ** -0.5),
        'w_f2': nrm(ks[23], (DEPTH, D_FF, D_MODEL), D_FF ** -0.5),
        'norm_ple': gain(ks[24], (DEPTH, D_MODEL)),
        'w_ple': nrm(ks[25], (DEPTH, PLE_DIM, D_MODEL), PLE_DIM ** -0.5),
        'w_ple_gate': nrm(ks[26], (DEPTH, D_MODEL, D_MODEL), D_MODEL ** -0.5),
        'final_norm': gain(ks[27], (D_MODEL,)),
    }


def reference(x_prompt, x_sample, p_prompt, p_sample, cache_k, cache_v, state_gla,
              w_in, norm_mix, lambda_q1, lambda_k1, lambda_q2, lambda_k2, a_subln, w_pa,
              w_ga2, b_ga, g_norm, w_pb, w_o, norm_ffn, w_f1, w_f3, w_f2, norm_ple,
              w_ple, w_ple_gate, final_norm):
    T_p = x_prompt.shape[1]
    DB, T_s = x_sample.shape[0], x_sample.shape[1]
    P = cache_k.shape[2]
    pos_p = jnp.arange(T_p)
    pos_s = P + jnp.arange(T_s)
    xp, xs = x_prompt, x_sample
    kp_l, vp_l, sp_l, ks_l, vs_l, ss_l = [], [], [], [], [], []
    for l in range(DEPTH):
        lw = (w_in[l], norm_mix[l], lambda_q1[l], lambda_k1[l], lambda_q2[l], lambda_k2[l],
              a_subln[l], w_pa[l], w_ga2[l], b_ga[l], g_norm[l], w_pb[l], w_o[l],
              norm_ffn[l], w_f1[l], w_f3[l], w_f2[l], norm_ple[l], w_ple[l], w_ple_gate[l])
        xp, kp, vp, sp = layer(xp, p_prompt[l], pos_p, l, diff_attend_prompt, gla_prompt, lw)
        kc = cache_k[l].reshape(DB, P, A_HEADS, 2, A_DH)
        vc = cache_v[l]

        def attend_s(q, k, v, lam, kc=kc, vc=vc):
            return diff_attend(q, jnp.concatenate([kc.astype(k.dtype), k], axis=1),
                               jnp.concatenate([vc.astype(v.dtype), v], axis=1), lam, None)

        def recur_s(q, k, v, g, s0=state_gla[l]):
            return gla_sample(q, k, v, g, s0)

        xs, kn, vn, sn = layer(xs, p_sample[l], pos_s, l, attend_s, recur_s, lw)
        kp_l.append(kp); vp_l.append(vp); sp_l.append(sp)
        ks_l.append(kn); vs_l.append(vn); ss_l.append(sn)
    y_prompt = rmsnorm(xp, final_norm)
    y_sample = rmsnorm(xs, final_norm)
    k_prompt = jnp.stack(kp_l)
    v_prompt = jnp.stack(vp_l)
    s_prompt = jnp.stack(sp_l)
    k_sample = jnp.stack(ks_l)
    v_sample = jnp.stack(vs_l)
    s_sample = jnp.stack(ss_l)
    return (y_prompt, y_sample, k_prompt, v_prompt, s_prompt, k_sample, v_sample, s_sample)
```

```python
import functools
import math

import numpy as np
import jax
import jax.numpy as jnp
from jax import lax
from jax.experimental import pallas as pl
from jax.experimental.pallas import tpu as pltpu

F32 = jnp.float32
BF16 = jnp.bfloat16

D_MODEL = 1024
CHUNK = 64
EPS = 1e-6
A_HEADS = 8
A_DH = 64
A_DV = 128
ROPE_DIM = 16
ROPE_THETA = 500000.0
G_HEADS = 4
G_DK = 128
G_DV = 256
G_KW = 512
G_RANK = 16
G_NORMALIZER = 16.0
PLE_DIM = 256
LANES = 128
VMEM_LIMIT = 56 * 1024 * 1024

NT_DIMS = (((1,), (1,)), ((), ()))


def _dot(a, b):
    return jnp.dot(a, b, preferred_element_type=F32)


def _dot_nt(a, b):
    return lax.dot_general(a, b, NT_DIMS, preferred_element_type=F32)


def _sigmoid(x):
    return 1.0 / (1.0 + jnp.exp(-x))


def _log_sigmoid(x):
    return jnp.minimum(x, 0.0) - jnp.log(1.0 + jnp.exp(-jnp.abs(x)))


def _rms(x, w):
    return x * lax.rsqrt(jnp.mean(x * x, axis=-1, keepdims=True) + EPS) * w


def _const_spec(shape):
    nd = len(shape)
    return pl.BlockSpec(shape, lambda *_: (0,) * nd, pipeline_mode=pl.Buffered(1))


def _params(sem):
    return pltpu.CompilerParams(dimension_semantics=sem, vmem_limit_bytes=VMEM_LIMIT)


def _in_proj_kernel(*refs, aliased):
    (x_ref, nw_ref, wm_ref, wkT_ref, wag_ref, wagT_ref, wga_ref, wgaT_ref, bga_ref, bgaT_ref,
     cos_ref, s1_ref, s2_ref) = refs[:13]
    outs = refs[13 + (2 if aliased else 0):]
    (q_ref, k_ref, v_ref, qkg_ref, vg_ref, rg_ref, g2_ref, kT_ref, gTh_ref, gTl_ref,
     sga_ref, sgb_ref) = outs

    h = _rms(x_ref[...], nw_ref[...]).astype(BF16)

    def mm(c0, c1):
        return _dot(h, wm_ref[:, c0:c1])

    cos, s1, s2 = cos_ref[...], s1_ref[...], s2_ref[...]

    def rope_store(z, out_ref, scale):
        for c in range(z.shape[1] // LANES):
            zc = z[:, c * LANES:(c + 1) * LANES]
            r = zc * cos + pltpu.roll(zc, 8, 1) * s1 + pltpu.roll(zc, LANES - 8, 1) * s2
            if scale != 1.0:
                r = r * scale
            out_ref[:, c * LANES:(c + 1) * LANES] = r.astype(out_ref.dtype)

    rope_store(mm(0, 1024), q_ref, A_DH ** -0.5)
    rope_store(mm(1024, 2048), k_ref, 1.0)
    v_ref[...] = mm(2048, 3072)
    z = mm(3072, 4096)
    qkg_ref[:, :G_KW] = (z[:, :G_KW] * (G_DK ** -0.5)).astype(BF16)
    qkg_ref[:, G_KW:] = z[:, G_KW:].astype(BF16)
    vg_ref[...] = mm(4096, 5120).astype(BF16)
    rg_ref[...] = mm(5120, 6144).astype(BF16)
    sga_ref[...] = _sigmoid(mm(6144, 7168)).astype(BF16)
    sgb_ref[...] = _sigmoid(mm(7168, 8192)).astype(BF16)

    ag = _dot(h, wag_ref[...]).astype(BF16)
    g = _log_sigmoid(_dot(ag, wga_ref[...]) + bga_ref[...]) * (1.0 / G_NORMALIZER)
    g_hi = g.astype(BF16)
    g2_ref[:, :G_KW] = g_hi
    g2_ref[:, G_KW:] = (g - g_hi.astype(F32)).astype(BF16)
    agT = _dot_nt(wagT_ref[...], h).astype(BF16)
    gT = _log_sigmoid(_dot(wgaT_ref[...], agT) + bgaT_ref[...]) * (1.0 / G_NORMALIZER)
    gT_hi = gT.astype(BF16)
    gTh_ref[...] = gT_hi
    gTl_ref[...] = (gT - gT_hi.astype(F32)).astype(BF16)
    kT_ref[...] = _dot_nt(wkT_ref[...], h).astype(BF16)


def _in_proj(x, lw, rope, layer, kbuf, vbuf, tm):
    n = x.shape[0]
    nt = n // tm
    cos, s1, s2 = rope
    nrb = cos.shape[0] // tm
    aliased = kbuf is not None
    depth = lw["depth"]

    row = lambda w: pl.BlockSpec((tm, w), lambda i: (i, 0))
    col = lambda h: pl.BlockSpec((h, tm), lambda i: (0, i))
    tab = pl.BlockSpec((tm, LANES), lambda i: (i % nrb, 0))
    in_specs = [row(D_MODEL), _const_spec((1, D_MODEL)), _const_spec((D_MODEL, 8192)),
                _const_spec((G_KW, D_MODEL)), _const_spec((D_MODEL, LANES)), _const_spec((LANES, D_MODEL)),
                _const_spec((LANES, G_KW)), _const_spec((G_KW, LANES)), _const_spec((1, G_KW)),
                _const_spec((G_KW, 1)), tab, tab, tab]
    args = [x, lw["norm_mix"], lw["w_main"], lw["w_kT"], lw["w_ag"], lw["w_agT"], lw["w_ga"], lw["w_gaT"],
            lw["b_ga"], lw["b_gaT"], cos, s1, s2]
    io_alias = {}
    if aliased:
        in_specs += [pl.BlockSpec(memory_space=pl.ANY)] * 2
        args += [kbuf, vbuf]
        io_alias = {13: 1, 14: 2}
    kv_spec = pl.BlockSpec((None, tm, D_MODEL), lambda i: (layer, i, 0))
    out_specs = [row(D_MODEL), kv_spec, kv_spec, row(1024), row(1024), row(1024), row(1024),
                 col(G_KW), col(G_KW), col(G_KW), row(D_MODEL), row(D_MODEL)]
    sd = jax.ShapeDtypeStruct
    out_shape = [sd((n, D_MODEL), BF16), sd((depth, n, D_MODEL), F32), sd((depth, n, D_MODEL), F32),
                 sd((n, 1024), BF16), sd((n, 1024), BF16), sd((n, 1024), BF16), sd((n, 1024), BF16),
                 sd((G_KW, n), BF16), sd((G_KW, n), BF16), sd((G_KW, n), BF16),
                 sd((n, D_MODEL), BF16), sd((n, D_MODEL), BF16)]
    return pl.pallas_call(
        functools.partial(_in_proj_kernel, aliased=aliased),
        grid=(nt,), in_specs=in_specs, out_specs=out_specs, out_shape=out_shape,
        input_output_aliases=io_alias, compiler_params=_params(("parallel",)), name="in_proj",
    )(*args)


def _lambda(lamp_ref, lam_init):
    lp = lamp_ref[...]
    d1 = jnp.sum(lp[0:1] * lp[1:2], axis=-1, keepdims=True)
    d2 = jnp.sum(lp[2:3] * lp[3:4], axis=-1, keepdims=True)
    return jnp.exp(d1) - jnp.exp(d2) + lam_init


def _attn_finish(o1, o2, lamp_ref, sub_ref, lam_init):
    o = o1 - _lambda(lamp_ref, lam_init) * o2
    return (_rms(o, sub_ref[...]) * (1.0 - lam_init)).astype(BF16)


def _attn_prompt_kernel(q_ref, k_ref, v_ref, lamp_ref, sub_ref, o_ref,
                        k1b, k2b, vb, m1, l1, a1, m2, l2, a2, *, tq, lam_init):
    qi = pl.program_id(2)

    @pl.when(qi == 0)
    def _():
        k = k_ref[...]
        k1b[...] = k[:, :A_DH].astype(BF16)
        k2b[...] = k[:, A_DH:].astype(BF16)
        vb[...] = v_ref[...].astype(BF16)

    q = q_ref[...]
    maps = ((q[:, :A_DH], k1b, m1, l1, a1), (q[:, A_DH:], k2b, m2, l2, a2))
    for _, _, m_ref, l_ref, a_ref in maps:
        m_ref[...] = jnp.full(m_ref.shape, -jnp.inf, F32)
        l_ref[...] = jnp.zeros(l_ref.shape, F32)
        a_ref[...] = jnp.zeros(a_ref.shape, F32)

    ri = lax.broadcasted_iota(jnp.int32, (tq, tq), 0) // CHUNK
    ci = lax.broadcasted_iota(jnp.int32, (tq, tq), 1) // CHUNK
    chunk_mask = ci <= ri

    def tile(t, masked):
        off = pl.multiple_of(t * tq, tq)
        vt = vb[pl.ds(off, tq), :]
        for qq, kb, m_ref, l_ref, a_ref in maps:
            s = _dot_nt(qq, kb[pl.ds(off, tq), :])
            if masked:
                s = jnp.where(chunk_mask, s, -jnp.inf)
            m_old = m_ref[...]
            m_new = jnp.maximum(m_old, jnp.max(s, axis=-1, keepdims=True))
            alpha = jnp.exp(m_old - m_new)
            p = jnp.exp(s - m_new)
            l_ref[...] = alpha * l_ref[...] + jnp.sum(p, axis=-1, keepdims=True)
            a_ref[...] = alpha * a_ref[...] + _dot(p.astype(BF16), vt)
            m_ref[...] = m_new

    def body(t, carry):
        tile(t, False)
        return carry

    lax.fori_loop(0, qi, body, 0)
    tile(qi, True)
    o_ref[...] = _attn_finish(a1[...] / l1[...], a2[...] / l2[...], lamp_ref, sub_ref, lam_init)


def _attn_prompt(q, kbuf, vbuf, lamp, sub, layer, batch, seq, tq):
    nq = seq // tq
    lam_init = 0.8 - 0.6 * math.exp(-0.3 * layer)
    kv_spec = pl.BlockSpec((None, seq, A_DV), lambda b, h, i: (layer, b, h))
    qo_spec = pl.BlockSpec((tq, A_DV), lambda b, h, i: (b * nq + i, h))
    vm = pltpu.VMEM
    return pl.pallas_call(
        functools.partial(_attn_prompt_kernel, tq=tq, lam_init=lam_init),
        grid=(batch, A_HEADS, nq),
        in_specs=[qo_spec, kv_spec, kv_spec,
                  pl.BlockSpec((8, LANES), lambda b, h, i: (0, 0)), pl.BlockSpec((1, A_DV), lambda b, h, i: (0, 0))],
        out_specs=qo_spec,
        out_shape=jax.ShapeDtypeStruct(q.shape, BF16),
        scratch_shapes=[vm((seq, A_DH), BF16), vm((seq, A_DH), BF16), vm((seq, A_DV), BF16),
                        vm((tq, 1), F32), vm((tq, 1), F32), vm((tq, A_DV), F32),
                        vm((tq, 1), F32), vm((tq, 1), F32), vm((tq, A_DV), F32)],
        compiler_params=_params(("parallel", "parallel", "arbitrary")), name="attn_prompt",
    )(q, kbuf, vbuf, lamp, sub)


def _attn_sample_kernel(q_ref, kc_ref, vc_ref, kn_ref, vn_ref, lamp_ref, sub_ref, o_ref, *, lam_init):
    q = q_ref[...]
    kc = kc_ref[...].astype(BF16)
    vc = vc_ref[...].astype(BF16)
    kn = kn_ref[...].astype(BF16)
    vn = vn_ref[...].astype(BF16)
    outs = []
    for lo in (0, A_DH):
        qq = q[:, lo:lo + A_DH]
        sc = _dot_nt(qq, kc[:, lo:lo + A_DH])
        sn = _dot_nt(qq, kn[:, lo:lo + A_DH])
        m = jnp.maximum(jnp.max(sc, axis=-1, keepdims=True), jnp.max(sn, axis=-1, keepdims=True))
        pc = jnp.exp(sc - m)
        pn = jnp.exp(sn - m)
        denom = jnp.sum(pc, axis=-1, keepdims=True) + jnp.sum(pn, axis=-1, keepdims=True)
        outs.append((_dot(pc.astype(BF16), vc) + _dot(pn.astype(BF16), vn)) / denom)
    o_ref[...] = _attn_finish(outs[0], outs[1], lamp_ref, sub_ref, lam_init)


def _attn_sample(q, cache_k, cache_v, kbuf, vbuf, lamp, sub, layer, batch, ts):
    past = cache_k.shape[2]
    lam_init = 0.8 - 0.6 * math.exp(-0.3 * layer)
    cache_spec = pl.BlockSpec((None, None, past, A_DV), lambda b, h: (layer, b, 0, h))
    new_spec = pl.BlockSpec((None, ts, A_DV), lambda b, h: (layer, b, h))
    qo_spec = pl.BlockSpec((ts, A_DV), lambda b, h: (b, h))
    return pl.pallas_call(
        functools.partial(_attn_sample_kernel, lam_init=lam_init),
        grid=(batch, A_HEADS),
        in_specs=[qo_spec, cache_spec, cache_spec, new_spec, new_spec,
                  pl.BlockSpec((8, LANES), lambda b, h: (0, 0)), pl.BlockSpec((1, A_DV), lambda b, h: (0, 0))],
        out_specs=qo_spec,
        out_shape=jax.ShapeDtypeStruct(q.shape, BF16),
        compiler_params=_params(("parallel", "parallel")), name="attn_sample",
    )(q, cache_k, cache_v, kbuf, vbuf, lamp, sub)


def _gla_tables(c):
    levels = int(math.log2(c))
    t = np.arange(c)[:, None]
    r = np.arange(c)[None, :]
    blocks = [(r <= t)]
    for p in range(levels):
        s = 1 << p
        m = (t >> (p + 1) << (p + 1)) + s - 1
        upper = ((t >> p) & 1) == 1
        blocks.append(np.where(upper, (r > m) & (r <= t), (r > t) & (r <= m)))
    l_rows = np.concatenate(blocks, axis=0).astype(np.float32)
    tail_t = (np.arange(c)[:, None] > np.arange(c)[None, :]).astype(np.float32)
    rt = np.concatenate([tail_t, np.ones((c, G_DV), np.float32)], axis=1)
    return (jnp.asarray(np.concatenate([l_rows, l_rows], axis=1), BF16),
            jnp.asarray(np.concatenate([rt, rt], axis=0), BF16), levels)


def _gla_kernel(*refs, c, nch, levels, has_s0, aliased):
    qk_ref, v_ref, r_ref, g2_ref, kT_ref, gTh_ref, gTl_ref, l2_ref, rt_ref, gn_ref = refs[:10]
    pos = 10
    s0_ref = None
    if has_s0:
        s0_ref = refs[pos]
        pos += 1
    if aliased:
        pos += 1
    o_ref, sout_ref, s_scr = refs[pos:pos + 3]
    t = pl.program_id(1)

    @pl.when(t == 0)
    def _():
        if has_s0:
            s_scr[...] = s0_ref[...]
        else:
            s_scr[...] = jnp.zeros(s_scr.shape, F32)

    row = lax.broadcasted_iota(jnp.int32, (c, G_DK), 0)
    ri = lax.broadcasted_iota(jnp.int32, (c, c), 0)
    ci = lax.broadcasted_iota(jnp.int32, (c, c), 1)
    upper = [((row >> p) & 1) == 1 for p in range(levels)]
    pair = [(((ri ^ ci) >> p) == 1) & (((ri >> p) & 1) == 1) for p in range(levels)]
    eye = ri == ci
    gn = gn_ref[...]

    def chunk(r0):
        rows = pl.ds(r0, c)
        g2 = g2_ref[rows, :]
        x_all = _dot(l2_ref[...], jnp.concatenate([g2[:, :G_KW], g2[:, G_KW:]], axis=0))
        gT = jnp.concatenate([gTh_ref[:, rows], gTl_ref[:, rows]], axis=1)
        xT = _dot(gT, rt_ref[...])
        for h in range(G_HEADS):
            dk = slice(h * G_DK, (h + 1) * G_DK)
            dv = slice(h * G_DV, (h + 1) * G_DV)
            qb = qk_ref[rows, dk]
            kb = qk_ref[rows, G_KW + h * G_DK:G_KW + (h + 1) * G_DK]
            q = qb.astype(F32)
            k = kb.astype(F32)
            v = v_ref[rows, dv]
            a = jnp.where(eye, _dot_nt(qb, kb), 0.0)
            for p in range(levels):
                e = jnp.exp(x_all[(1 + p) * c:(2 + p) * c, dk])
                w = (jnp.where(upper[p], q, k) * e).astype(BF16)
                a = a + jnp.where(pair[p], _dot_nt(w, w), 0.0)
            s_old = s_scr[h]
            qe = (q * jnp.exp(x_all[0:c, dk])).astype(BF16)
            o = _dot(qe, s_old.astype(BF16)) + _dot(a.astype(BF16), v)
            rg = r_ref[rows, dv].astype(F32)
            o_ref[rows, dv] = (_rms(o, gn) * (rg * _sigmoid(rg))).astype(BF16)
            kT = kT_ref[dk, rows].astype(F32)
            kdT = (kT * jnp.exp(xT[dk, 0:c])).astype(BF16)
            s_scr[h] = jnp.exp(xT[dk, c:c + G_DV]) * s_old + _dot(kdT, v)

    if nch == 1:
        chunk(0)
    else:
        def body(i, carry):
            chunk(pl.multiple_of(i * c, c))
            return carry
        lax.fori_loop(0, nch, body, 0)

    @pl.when(t == pl.num_programs(1) - 1)
    def _():
        sout_ref[...] = s_scr[...]


def _gla(qkg, vg, rg, g2, kT, gTh, gTl, gn, s0, sbuf, layer, depth, batch, seq, c, cg):
    n = batch * seq
    nb = seq // cg
    l2, rt, levels = _gla_tables(c)
    has_s0 = s0 is not None
    aliased = sbuf is not None
    row = pl.BlockSpec((cg, 1024), lambda b, t: (b * nb + t, 0))
    if kT.ndim == 3:
        col = pl.BlockSpec((None, G_KW, cg), lambda b, t: (b, 0, t))
    else:
        col = pl.BlockSpec((G_KW, cg), lambda b, t: (0, b * nb + t))
    in_specs = [row, row, row, row, col, col, col, _const_spec(l2.shape), _const_spec(rt.shape),
                _const_spec((1, G_DV))]
    args = [qkg, vg, rg, g2, kT, gTh, gTl, l2, rt, gn]
    if has_s0:
        in_specs.append(pl.BlockSpec((None, None, G_HEADS, G_DK, G_DV), lambda b, t: (layer, b, 0, 0, 0)))
        args.append(s0)
    io_alias = {}
    if aliased:
        in_specs.append(pl.BlockSpec(memory_space=pl.ANY))
        args.append(sbuf)
        io_alias = {len(args) - 1: 1}
    s_spec = pl.BlockSpec((None, None, G_HEADS, G_DK, G_DV), lambda b, t: (layer, b, 0, 0, 0))
    return pl.pallas_call(
        functools.partial(_gla_kernel, c=c, nch=cg // c, levels=levels, has_s0=has_s0, aliased=aliased),
        grid=(batch, nb), in_specs=in_specs, out_specs=[row, s_spec],
        out_shape=[jax.ShapeDtypeStruct((n, 1024), BF16),
                   jax.ShapeDtypeStruct((depth, batch, G_HEADS, G_DK, G_DV), F32)],
        scratch_shapes=[pltpu.VMEM((G_HEADS, G_DK, G_DV), F32)],
        input_output_aliases=io_alias,
        compiler_params=_params(("parallel", "arbitrary")), name="gla",
    )(*args)


def _merge_kernel(x_ref, oa_ref, og_ref, sga_ref, sgb_ref, wpa_ref, wpb_ref, wo_ref, o_ref):
    ua = _dot(oa_ref[...], wpa_ref[...])
    ub = _dot(og_ref[...], wpb_ref[...])
    mixed = sga_ref[...].astype(F32) * ua + sgb_ref[...].astype(F32) * ub
    o_ref[...] = x_ref[...] + _dot(mixed.astype(BF16), wo_ref[...])


def _merge(x, oa, og, sga, sgb, lw, tm):
    n = x.shape[0]
    row = pl.BlockSpec((tm, D_MODEL), lambda i: (i, 0))
    wspec = _const_spec((D_MODEL, D_MODEL))
    return pl.pallas_call(
        _merge_kernel, grid=(n // tm,),
        in_specs=[row, row, row, row, row, wspec, wspec, wspec], out_specs=row,
        out_shape=jax.ShapeDtypeStruct(x.shape, F32),
        compiler_params=_params(("parallel",)), name="merge",
    )(x, oa, og, sga, sgb, lw["w_pa"], lw["w_pb"], lw["w_o"])


def _ffn_ple_kernel(*refs, ff_chunks, final):
    (x_ref, pe_ref, nf_ref, w1_ref, w3_ref, w2_ref, np_ref, wpg_ref, wpe_ref) = refs[:9]
    fn_ref = refs[9] if final else None
    o_ref = refs[-1]
    x = x_ref[...]
    h = _rms(x, nf_ref[...]).astype(BF16)
    acc = x
    for c0, c1 in ff_chunks:
        a = _dot(h, w1_ref[:, c0:c1])
        u = (a * _sigmoid(a)) * _dot(h, w3_ref[:, c0:c1])
        acc = acc + _dot(u.astype(BF16), w2_ref[c0:c1, :])
    h2 = _rms(acc, np_ref[...]).astype(BF16)
    gate = _sigmoid(_dot(h2, wpg_ref[...]))
    y = acc + gate * _dot(pe_ref[...].astype(BF16), wpe_ref[...])
    if final:
        y = _rms(y, fn_ref[...])
    o_ref[...] = y


def _ffn_ple(x, pe, lw, final_norm, layer, tm):
    n = x.shape[0]
    d_ff = lw["w_f1"].shape[1]
    half = d_ff // 2
    ff_chunks = ((0, half), (half, d_ff)) if half % LANES == 0 else ((0, d_ff),)
    final = final_norm is not None
    row = pl.BlockSpec((tm, D_MODEL), lambda i: (i, 0))
    in_specs = [row, pl.BlockSpec((None, tm, PLE_DIM), lambda i: (layer, i, 0)), _const_spec((1, D_MODEL)),
                _const_spec((D_MODEL, d_ff)), _const_spec((D_MODEL, d_ff)), _const_spec((d_ff, D_MODEL)),
                _const_spec((1, D_MODEL)), _const_spec((D_MODEL, D_MODEL)), _const_spec((PLE_DIM, D_MODEL))]
    args = [x, pe, lw["norm_ffn"], lw["w_f1"], lw["w_f3"], lw["w_f2"], lw["norm_ple"], lw["w_ple_gate"],
            lw["w_ple"]]
    if final:
        in_specs.append(_const_spec((1, D_MODEL)))
        args.append(final_norm)
    return pl.pallas_call(
        functools.partial(_ffn_ple_kernel, ff_chunks=ff_chunks, final=final), grid=(n // tm,),
        in_specs=in_specs, out_specs=row, out_shape=jax.ShapeDtypeStruct(x.shape, F32),
        compiler_params=_params(("parallel",)), name="ffn_ple",
    )(*args)


def _rope_tables(pos):
    half = ROPE_DIM // 2
    inv = ROPE_THETA ** (-jnp.arange(0, ROPE_DIM, 2, dtype=F32) / ROPE_DIM)
    ang = pos.astype(F32)[:, None] * inv[None, :]
    cos, sin = jnp.cos(ang), jnp.sin(ang)
    n = pos.shape[0]
    pad = jnp.zeros((n, A_DH - ROPE_DIM), F32)
    cos64 = jnp.concatenate([cos, cos, pad + 1.0], axis=1)
    s1_64 = jnp.concatenate([jnp.zeros((n, half), F32), sin, pad], axis=1)
    s2_64 = jnp.concatenate([-sin, jnp.zeros((n, half), F32), pad], axis=1)
    return tuple(jnp.concatenate([t, t], axis=1) for t in (cos64, s1_64, s2_64))


def _prep_weights(w_in, norm_mix, w_ga2, b_ga, w_pa, w_pb, w_o, norm_ffn, w_f1, w_f3, w_f2, norm_ple, w_ple,
                  w_ple_gate):
    depth = w_in.shape[0]
    a0 = 6144
    w_main = jnp.concatenate([w_in[:, :, :a0], w_in[:, :, a0 + G_RANK:]], axis=2).astype(BF16)
    w_ag = jnp.pad(w_in[:, :, a0:a0 + G_RANK], ((0, 0), (0, 0), (0, LANES - G_RANK))).astype(BF16)
    w_ga = jnp.pad(w_ga2, ((0, 0), (0, LANES - G_RANK), (0, 0))).astype(BF16)
    layers = []
    for l in range(depth):
        layers.append(dict(
            depth=depth,
            norm_mix=norm_mix[l][None], w_main=w_main[l], w_kT=w_main[l][:, 3584:4096].T,
            w_ag=w_ag[l], w_agT=w_ag[l].T, w_ga=w_ga[l], w_gaT=w_ga[l].T,
            b_ga=b_ga[l][None], b_gaT=b_ga[l][:, None],
            w_pa=w_pa[l].astype(BF16), w_pb=w_pb[l].astype(BF16), w_o=w_o[l].astype(BF16),
            norm_ffn=norm_ffn[l][None], w_f1=w_f1[l].astype(BF16), w_f3=w_f3[l].astype(BF16),
            w_f2=w_f2[l].astype(BF16), norm_ple=norm_ple[l][None], w_ple=w_ple[l].astype(BF16),
            w_ple_gate=w_ple_gate[l].astype(BF16)))
    return layers


def kernel(x_prompt, x_sample, p_prompt, p_sample, cache_k, cache_v, state_gla, w_in, norm_mix, lambda_q1,
           lambda_k1, lambda_q2, lambda_k2, a_subln, w_pa, w_ga2, b_ga, g_norm, w_pb, w_o, norm_ffn, w_f1, w_f3,
           w_f2, norm_ple, w_ple, w_ple_gate, final_norm):
    depth = w_in.shape[0]
    batch, seq, _ = x_prompt.shape
    dbatch, ts, _ = x_sample.shape
    past = cache_k.shape[2]
    n_p, n_s = batch * seq, dbatch * ts
    assert ts == CHUNK and seq % 512 == 0 and n_s % 8 == 0

    layers = _prep_weights(w_in, norm_mix, w_ga2, b_ga, w_pa, w_pb, w_o, norm_ffn, w_f1, w_f3, w_f2, norm_ple,
                           w_ple, w_ple_gate)
    rope_p = _rope_tables(jnp.arange(seq))
    rope_s = _rope_tables(jnp.tile(past + jnp.arange(ts), dbatch))
    lamp = jnp.pad(jnp.stack([lambda_q1, lambda_k1, lambda_q2, lambda_k2], axis=1),
                   ((0, 0), (0, 4), (0, LANES - A_DH)))
    fnorm = final_norm[None]
    tm_p = 512
    tm_s = min(512, n_s)
    pe_p = p_prompt.reshape(depth, n_p, PLE_DIM)
    pe_s = p_sample.reshape(depth, n_s, PLE_DIM)
    cache_k4 = cache_k.reshape(depth, dbatch, past, A_HEADS * A_DV)
    cache_v4 = cache_v.reshape(depth, dbatch, past, A_HEADS * A_DV)

    xp = x_prompt.reshape(n_p, D_MODEL)
    xs = x_sample.reshape(n_s, D_MODEL)
    kp = vp = sp = ks = vs = ss = None
    for l in range(depth):
        lw = layers[l]
        last = fnorm if l == depth - 1 else None
        sub = a_subln[l][None]
        gn = g_norm[l][None]

        q, kp, vp, qkg, vg, rg, g2, kT, gTh, gTl, sga, sgb = _in_proj(xp, lw, rope_p, l, kp, vp, tm_p)
        oa = _attn_prompt(q, kp, vp, lamp[l], sub, l, batch, seq, 256)
        og, sp = _gla(qkg, vg, rg, g2, kT, gTh, gTl, gn, None, sp, l, depth, batch, seq, 128, 512)
        xp = _merge(xp, oa, og, sga, sgb, lw, tm_p)
        xp = _ffn_ple(xp, pe_p, lw, last, l, tm_p)

        q, ks, vs, qkg, vg, rg, g2, kT, gTh, gTl, sga, sgb = _in_proj(xs, lw, rope_s, l, ks, vs, tm_s)
        oa = _attn_sample(q, cache_k4, cache_v4, ks, vs, lamp[l], sub, l, dbatch, ts)
        by_batch = lambda a: a.reshape(G_KW, dbatch, ts).transpose(1, 0, 2)
        og, ss = _gla(qkg, vg, rg, g2, by_batch(kT), by_batch(gTh), by_batch(gTl), gn, state_gla, ss, l, depth,
                      dbatch, ts, ts, ts)
        xs = _merge(xs, oa, og, sga, sgb, lw, tm_s)
        xs = _ffn_ple(xs, pe_s, lw, last, l, tm_s)

    shape_kv = lambda a, b, t: a.reshape(depth, b, t, A_HEADS, A_DV)
    return (xp.reshape(batch, seq, D_MODEL), xs.reshape(dbatch, ts, D_MODEL),
            shape_kv(kp, batch, seq), shape_kv(vp, batch, seq), sp,
            shape_kv(ks, dbatch, ts), shape_kv(vs, dbatch, ts), ss)
```

```python
import functools
import math

import numpy as np
import jax
import jax.numpy as jnp
from jax import lax
from jax.experimental import pallas as pl
from jax.experimental.pallas import tpu as pltpu

F32 = jnp.float32
BF16 = jnp.bfloat16

D_MODEL = 1024
CHUNK = 64
EPS = 1e-6
A_HEADS = 8
A_DH = 64
A_DV = 128
ROPE_DIM = 16
ROPE_THETA = 500000.0
G_HEADS = 4
G_DK = 128
G_DV = 256
G_KW = 512
G_RANK = 16
G_NORMALIZER = 16.0
PLE_DIM = 256
LANES = 128
ONES_ROWS = 16
VMEM_LIMIT = 56 * 1024 * 1024

Q_SCALE = A_DH ** -0.5 * math.log2(math.e)
G_SCALE = math.log2(math.e) / G_NORMALIZER
NT_DIMS = (((1,), (1,)), ((), ()))


def _dot(a, b):
    return jnp.dot(a, b, preferred_element_type=F32)


def _dot_nt(a, b):
    return lax.dot_general(a, b, NT_DIMS, preferred_element_type=F32)


def _sigmoid(x):
    return 1.0 / (1.0 + jnp.exp(-x))


def _log_sigmoid(x):
    return jnp.minimum(x, 0.0) - jnp.log(1.0 + jnp.exp(-jnp.abs(x)))


def _rms(x, w):
    return x * lax.rsqrt(jnp.mean(x * x, axis=-1, keepdims=True) + EPS) * w


def _const_spec(shape):
    nd = len(shape)
    return pl.BlockSpec(shape, lambda *_: (0,) * nd, pipeline_mode=pl.Buffered(1))


def _params(sem):
    return pltpu.CompilerParams(dimension_semantics=sem, vmem_limit_bytes=VMEM_LIMIT)


def _in_proj_kernel(*refs, aliased):
    x_ref, nw_ref, wm_ref, wag_ref, wga_ref, bga_ref, cos_ref, s1_ref, s2_ref = refs[:9]
    outs = refs[9 + (2 if aliased else 0):]
    (q_ref, k_ref, v_ref, qkg_ref, vg_ref, rg_ref, g2_ref, kT_ref, gTh_ref, gTl_ref,
     sga_ref, sgb_ref) = outs

    h = _rms(x_ref[...], nw_ref[...]).astype(BF16)

    def mm(c0, c1):
        return _dot(h, wm_ref[:, c0:c1])

    cos, s1, s2 = cos_ref[...], s1_ref[...], s2_ref[...]

    def rope_store(z, out_ref, scale):
        for c in range(z.shape[1] // LANES):
            zc = z[:, c * LANES:(c + 1) * LANES]
            r = zc * cos + pltpu.roll(zc, 8, 1) * s1 + pltpu.roll(zc, LANES - 8, 1) * s2
            if scale != 1.0:
                r = r * scale
            out_ref[:, c * LANES:(c + 1) * LANES] = r.astype(out_ref.dtype)

    rope_store(mm(0, 1024), q_ref, Q_SCALE)
    rope_store(mm(1024, 2048), k_ref, 1.0)
    v_ref[...] = mm(2048, 3072)
    z = mm(3072, 4096)
    qkg_ref[:, :G_KW] = (z[:, :G_KW] * (G_DK ** -0.5)).astype(BF16)
    qkg_ref[:, G_KW:] = z[:, G_KW:].astype(BF16)
    kT_ref[...] = z[:, G_KW:].T.astype(BF16)
    vg_ref[...] = mm(4096, 5120).astype(BF16)
    rg_ref[...] = mm(5120, 6144).astype(BF16)
    sga_ref[...] = _sigmoid(mm(6144, 7168)).astype(BF16)
    sgb_ref[...] = _sigmoid(mm(7168, 8192)).astype(BF16)

    ag = _dot(h, wag_ref[...]).astype(BF16)
    g = _log_sigmoid(_dot(ag, wga_ref[...]) + bga_ref[...]) * G_SCALE
    g_hi = g.astype(BF16)
    g2_ref[:, :G_KW] = g_hi
    g2_ref[:, G_KW:] = (g - g_hi.astype(F32)).astype(BF16)
    gT = g.T
    gT_hi = gT.astype(BF16)
    gTh_ref[...] = gT_hi
    gTl_ref[...] = (gT - gT_hi.astype(F32)).astype(BF16)


def _in_proj(x, lw, rope, layer, kbuf, vbuf, tm):
    n = x.shape[0]
    nt = n // tm
    cos, s1, s2 = rope
    nrb = cos.shape[0] // tm
    aliased = kbuf is not None
    depth = lw["depth"]

    row = lambda w: pl.BlockSpec((tm, w), lambda i: (i, 0))
    col = lambda h: pl.BlockSpec((h, tm), lambda i: (0, i))
    tab = pl.BlockSpec((tm, LANES), lambda i: (i % nrb, 0))
    in_specs = [row(D_MODEL), _const_spec((1, D_MODEL)), _const_spec((D_MODEL, 8192)),
                _const_spec((D_MODEL, LANES)), _const_spec((LANES, G_KW)), _const_spec((1, G_KW)), tab, tab, tab]
    args = [x, lw["norm_mix"], lw["w_main"], lw["w_ag"], lw["w_ga"], lw["b_ga"], cos, s1, s2]
    io_alias = {}
    if aliased:
        in_specs += [pl.BlockSpec(memory_space=pl.ANY)] * 2
        args += [kbuf, vbuf]
        io_alias = {9: 1, 10: 2}
    kv_spec = pl.BlockSpec((None, tm, D_MODEL), lambda i: (layer, i, 0))
    out_specs = [row(D_MODEL), kv_spec, kv_spec, row(1024), row(1024), row(1024), row(1024),
                 col(G_KW), col(G_KW), col(G_KW), row(D_MODEL), row(D_MODEL)]
    sd = jax.ShapeDtypeStruct
    out_shape = [sd((n, D_MODEL), BF16), sd((depth, n, D_MODEL), F32), sd((depth, n, D_MODEL), F32),
                 sd((n, 1024), BF16), sd((n, 1024), BF16), sd((n, 1024), BF16), sd((n, 1024), BF16),
                 sd((G_KW, n), BF16), sd((G_KW, n), BF16), sd((G_KW, n), BF16),
                 sd((n, D_MODEL), BF16), sd((n, D_MODEL), BF16)]
    return pl.pallas_call(
        functools.partial(_in_proj_kernel, aliased=aliased),
        grid=(nt,), in_specs=in_specs, out_specs=out_specs, out_shape=out_shape,
        input_output_aliases=io_alias, compiler_params=_params(("parallel",)), name="in_proj",
    )(*args)


def _lambda(lamp_ref, lam_init):
    lp = lamp_ref[...]
    d1 = jnp.sum(lp[0:1] * lp[1:2], axis=-1, keepdims=True)
    d2 = jnp.sum(lp[2:3] * lp[3:4], axis=-1, keepdims=True)
    return jnp.exp(d1) - jnp.exp(d2) + lam_init


def _attn_finish(o1, o2, lamp_ref, sub_ref, lam_init):
    o = o1 - _lambda(lamp_ref, lam_init) * o2
    return (_rms(o, sub_ref[...]) * (1.0 - lam_init)).astype(BF16)


def _attn_seq_kernel(q_ref, k_ref, v_ref, lamp_ref, subT_ref, o_ref, kb, vT, qT, m_scr, acc, s_scr,
                     *, tq, nh, lam_init):
    seq = k_ref.shape[0]
    nq = seq // tq
    for h in range(nh):
        k = k_ref[:, h * A_DV:(h + 1) * A_DV]
        kb[h, 0] = k[:, :A_DH].astype(BF16)
        kb[h, 1] = k[:, A_DH:].astype(BF16)
        vT[h, 0:A_DV, :] = v_ref[:, h * A_DV:(h + 1) * A_DV].T.astype(BF16)
        vT[h, A_DV:, :] = jnp.ones((ONES_ROWS, seq), BF16)
        qt = q_ref[:, h * A_DV:(h + 1) * A_DV].astype(F32).T
        qT[h, 0] = qt[:A_DH].astype(BF16)
        qT[h, 1] = qt[A_DH:].astype(BF16)

    key_chunk = lax.broadcasted_iota(jnp.int32, (tq, tq), 0) // CHUNK
    qry_chunk = lax.broadcasted_iota(jnp.int32, (tq, tq), 1) // CHUNK
    lam = _lambda(lamp_ref, lam_init)
    tiles = [(qi, t) for qi in range(nq) for t in range(qi + 1)]

    def scores(n):
        qi, t = tiles[n]
        for h in range(nh):
            for mp in range(2):
                s_scr[n % 2, 2 * h + mp] = _dot(kb[h, mp, t * tq:(t + 1) * tq, :],
                                                qT[h, mp, :, qi * tq:(qi + 1) * tq])

    def softmax_pv(n):
        qi, t = tiles[n]
        for h in range(nh):
            vt = vT[h, :, t * tq:(t + 1) * tq]
            for mp in range(2):
                c = 2 * h + mp
                s = s_scr[n % 2, c]
                if t == qi:
                    s = jnp.where(key_chunk <= qry_chunk, s, -jnp.inf)
                pv = lambda p: _dot(vt, p.astype(BF16))
                if t == 0:
                    m_new = jnp.max(s, axis=0, keepdims=True)
                    acc[c] = pv(jnp.exp2(s - m_new))
                else:
                    m_old = m_scr[c]
                    m_new = jnp.maximum(m_old, jnp.max(s, axis=0, keepdims=True))
                    acc[c] = jnp.exp2(m_old - m_new) * acc[c] + pv(jnp.exp2(s - m_new))
                m_scr[c] = m_new

    def finish(qi):
        for h in range(nh):
            a1, a2 = acc[2 * h], acc[2 * h + 1]
            oT = a1[:A_DV] / a1[A_DV:A_DV + 1] - lam * (a2[:A_DV] / a2[A_DV:A_DV + 1])
            ms = jnp.mean(oT * oT, axis=0, keepdims=True)
            yT = oT * lax.rsqrt(ms + EPS) * subT_ref[...] * (1.0 - lam_init)
            o_ref[qi * tq:(qi + 1) * tq, h * A_DV:(h + 1) * A_DV] = yT.T.astype(BF16)

    scores(0)
    for n, (qi, t) in enumerate(tiles):
        if n + 1 < len(tiles):
            scores(n + 1)
        softmax_pv(n)
        if t == qi:
            finish(qi)


def _attn_seq(q, kbuf, vbuf, lamp, subT, layer, batch, seq, tq, nh):
    lam_init = 0.8 - 0.6 * math.exp(-0.3 * layer)
    kv_spec = pl.BlockSpec((None, seq, nh * A_DV), lambda b, h: (layer, b, h))
    qo_spec = pl.BlockSpec((seq, nh * A_DV), lambda b, h: (b, h))
    vm = pltpu.VMEM
    return pl.pallas_call(
        functools.partial(_attn_seq_kernel, tq=tq, nh=nh, lam_init=lam_init),
        grid=(batch, A_HEADS // nh),
        in_specs=[qo_spec, kv_spec, kv_spec,
                  pl.BlockSpec((8, LANES), lambda b, h: (0, 0)), pl.BlockSpec((A_DV, 1), lambda b, h: (0, 0))],
        out_specs=qo_spec,
        out_shape=jax.ShapeDtypeStruct(q.shape, BF16),
        scratch_shapes=[vm((nh, 2, seq, A_DH), BF16), vm((nh, A_DV + ONES_ROWS, seq), BF16),
                        vm((nh, 2, A_DH, seq), BF16), vm((2 * nh, 1, tq), F32),
                        vm((2 * nh, A_DV + ONES_ROWS, tq), F32), vm((2, 2 * nh, tq, tq), F32)],
        compiler_params=_params(("parallel", "parallel")), name="attn_prompt",
    )(q, kbuf, vbuf, lamp, subT)


def _attn_sample_kernel(q_ref, kc_ref, vc_ref, kn_ref, vn_ref, lamp_ref, sub_ref, o_ref, *, lam_init):
    q = q_ref[...]
    kc = kc_ref[...].astype(BF16)
    vc = vc_ref[...].astype(BF16)
    kn = kn_ref[...].astype(BF16)
    vn = vn_ref[...].astype(BF16)
    outs = []
    for lo in (0, A_DH):
        qq = q[:, lo:lo + A_DH]
        sc = _dot_nt(qq, kc[:, lo:lo + A_DH])
        sn = _dot_nt(qq, kn[:, lo:lo + A_DH])
        m = jnp.maximum(jnp.max(sc, axis=-1, keepdims=True), jnp.max(sn, axis=-1, keepdims=True))
        pc = jnp.exp2(sc - m)
        pn = jnp.exp2(sn - m)
        denom = jnp.sum(pc, axis=-1, keepdims=True) + jnp.sum(pn, axis=-1, keepdims=True)
        outs.append((_dot(pc.astype(BF16), vc) + _dot(pn.astype(BF16), vn)) / denom)
    o_ref[...] = _attn_finish(outs[0], outs[1], lamp_ref, sub_ref, lam_init)


def _attn_sample(q, cache_k, cache_v, kbuf, vbuf, lamp, sub, layer, batch, ts):
    past = cache_k.shape[2]
    lam_init = 0.8 - 0.6 * math.exp(-0.3 * layer)
    cache_spec = pl.BlockSpec((None, None, past, A_DV), lambda b, h: (layer, b, 0, h))
    new_spec = pl.BlockSpec((None, ts, A_DV), lambda b, h: (layer, b, h))
    qo_spec = pl.BlockSpec((ts, A_DV), lambda b, h: (b, h))
    return pl.pallas_call(
        functools.partial(_attn_sample_kernel, lam_init=lam_init),
        grid=(batch, A_HEADS),
        in_specs=[qo_spec, cache_spec, cache_spec, new_spec, new_spec,
                  pl.BlockSpec((8, LANES), lambda b, h: (0, 0)), pl.BlockSpec((1, A_DV), lambda b, h: (0, 0))],
        out_specs=qo_spec,
        out_shape=jax.ShapeDtypeStruct(q.shape, BF16),
        compiler_params=_params(("parallel", "parallel")), name="attn_sample",
    )(q, cache_k, cache_v, kbuf, vbuf, lamp, sub)


def _gla_tables(c):
    levels = int(math.log2(c))
    t = np.arange(c)[:, None]
    r = np.arange(c)[None, :]
    blocks = [(r <= t)]
    for p in range(levels):
        s = 1 << p
        m = (t >> (p + 1) << (p + 1)) + s - 1
        upper = ((t >> p) & 1) == 1
        blocks.append(np.where(upper, (r > m) & (r <= t), (r > t) & (r <= m)))
    l_rows = np.concatenate(blocks, axis=0).astype(np.float32)
    tail_t = (np.arange(c)[:, None] > np.arange(c)[None, :]).astype(np.float32)
    rt = np.concatenate([tail_t, np.ones((c, G_DV), np.float32)], axis=1)
    i, j = t, r
    masks = [i == j] + [(((i ^ j) >> p) == 1) & (((i >> p) & 1) == 1) for p in range(levels)]
    return (jnp.asarray(np.concatenate([l_rows, l_rows], axis=1), BF16),
            jnp.asarray(np.concatenate([rt, rt], axis=0), BF16),
            jnp.asarray(np.stack(masks).astype(np.float32)), levels)


def _gla_kernel(*refs, c, nch, levels, has_s0, aliased):
    qk_ref, v_ref, r_ref, g2_ref, kT_ref, gTh_ref, gTl_ref, l2_ref, rt_ref, pm_ref, gn_ref = refs[:11]
    pos = 11
    s0_ref = None
    if has_s0:
        s0_ref = refs[pos]
        pos += 1
    if aliased:
        pos += 1
    o_ref, sout_ref, s_scr = refs[pos:pos + 3]
    t = pl.program_id(1)

    @pl.when(t == 0)
    def _():
        if has_s0:
            s_scr[...] = s0_ref[...]
        else:
            s_scr[...] = jnp.zeros(s_scr.shape, F32)

    gn = gn_ref[...]

    def chunk(r0):
        rows = pl.ds(r0, c)
        g2 = g2_ref[rows, :]
        x_all = _dot(l2_ref[...], jnp.concatenate([g2[:, :G_KW], g2[:, G_KW:]], axis=0))
        gT = jnp.concatenate([gTh_ref[:, rows], gTl_ref[:, rows]], axis=1)
        xT = _dot(gT, rt_ref[...])
        for h in range(G_HEADS):
            dk = slice(h * G_DK, (h + 1) * G_DK)
            dv = slice(h * G_DV, (h + 1) * G_DV)
            qb = qk_ref[rows, dk]
            kb = qk_ref[rows, G_KW + h * G_DK:G_KW + (h + 1) * G_DK]
            q = qb.astype(F32)
            k = kb.astype(F32)
            v = v_ref[rows, dv]
            a = pm_ref[0] * _dot_nt(qb, kb)
            for p in range(levels):
                e = jnp.exp2(x_all[(1 + p) * c:(2 + p) * c, dk])
                a = a + pm_ref[1 + p] * _dot_nt((q * e).astype(BF16), (k * e).astype(BF16))
            s_old = s_scr[h]
            qe = (q * jnp.exp2(x_all[0:c, dk])).astype(BF16)
            o = _dot(qe, s_old.astype(BF16)) + _dot(a.astype(BF16), v)
            rg = r_ref[rows, dv].astype(F32)
            o_ref[rows, dv] = (_rms(o, gn) * (rg * _sigmoid(rg))).astype(BF16)
            kT = kT_ref[dk, rows].astype(F32)
            kdT = (kT * jnp.exp2(xT[dk, 0:c])).astype(BF16)
            s_scr[h] = jnp.exp2(xT[dk, c:c + G_DV]) * s_old + _dot(kdT, v)

    if nch == 1:
        chunk(0)
    else:
        def body(i, carry):
            chunk(pl.multiple_of(i * c, c))
            return carry
        lax.fori_loop(0, nch, body, 0)

    @pl.when(t == pl.num_programs(1) - 1)
    def _():
        sout_ref[...] = s_scr[...]


def _gla(qkg, vg, rg, g2, kT, gTh, gTl, gn, s0, sbuf, layer, depth, batch, seq, c, cg):
    n = batch * seq
    nb = seq // cg
    l2, rt, pmask, levels = _gla_tables(c)
    has_s0 = s0 is not None
    aliased = sbuf is not None
    row = pl.BlockSpec((cg, 1024), lambda b, t: (b * nb + t, 0))
    if kT.ndim == 3:
        col = pl.BlockSpec((None, G_KW, cg), lambda b, t: (b, 0, t))
    else:
        col = pl.BlockSpec((G_KW, cg), lambda b, t: (0, b * nb + t))
    in_specs = [row, row, row, row, col, col, col, _const_spec(l2.shape), _const_spec(rt.shape),
                _const_spec(pmask.shape), _const_spec((1, G_DV))]
    args = [qkg, vg, rg, g2, kT, gTh, gTl, l2, rt, pmask, gn]
    if has_s0:
        in_specs.append(pl.BlockSpec((None, None, G_HEADS, G_DK, G_DV), lambda b, t: (layer, b, 0, 0, 0)))
        args.append(s0)
    io_alias = {}
    if aliased:
        in_specs.append(pl.BlockSpec(memory_space=pl.ANY))
        args.append(sbuf)
        io_alias = {len(args) - 1: 1}
    s_spec = pl.BlockSpec((None, None, G_HEADS, G_DK, G_DV), lambda b, t: (layer, b, 0, 0, 0))
    return pl.pallas_call(
        functools.partial(_gla_kernel, c=c, nch=cg // c, levels=levels, has_s0=has_s0, aliased=aliased),
        grid=(batch, nb), in_specs=in_specs, out_specs=[row, s_spec],
        out_shape=[jax.ShapeDtypeStruct((n, 1024), BF16),
                   jax.ShapeDtypeStruct((depth, batch, G_HEADS, G_DK, G_DV), F32)],
        scratch_shapes=[pltpu.VMEM((G_HEADS, G_DK, G_DV), F32)],
        input_output_aliases=io_alias,
        compiler_params=_params(("parallel", "arbitrary")), name="gla",
    )(*args)


def _merge_kernel(x_ref, oa_ref, og_ref, sga_ref, sgb_ref, wpa_ref, wpb_ref, wo_ref, o_ref):
    ua = _dot(oa_ref[...], wpa_ref[...])
    ub = _dot(og_ref[...], wpb_ref[...])
    mixed = sga_ref[...].astype(F32) * ua + sgb_ref[...].astype(F32) * ub
    o_ref[...] = x_ref[...] + _dot(mixed.astype(BF16), wo_ref[...])


def _merge(x, oa, og, sga, sgb, lw, tm):
    n = x.shape[0]
    row = pl.BlockSpec((tm, D_MODEL), lambda i: (i, 0))
    wspec = _const_spec((D_MODEL, D_MODEL))
    return pl.pallas_call(
        _merge_kernel, grid=(n // tm,),
        in_specs=[row, row, row, row, row, wspec, wspec, wspec], out_specs=row,
        out_shape=jax.ShapeDtypeStruct(x.shape, F32),
        compiler_params=_params(("parallel",)), name="merge",
    )(x, oa, og, sga, sgb, lw["w_pa"], lw["w_pb"], lw["w_o"])


def _ffn_ple_kernel(*refs, ff_chunks, final):
    (x_ref, pe_ref, nf_ref, w1_ref, w3_ref, w2_ref, np_ref, wpg_ref, wpe_ref) = refs[:9]
    fn_ref = refs[9] if final else None
    o_ref = refs[-1]
    x = x_ref[...]
    h = _rms(x, nf_ref[...]).astype(BF16)
    acc = x
    for c0, c1 in ff_chunks:
        a = _dot(h, w1_ref[:, c0:c1])
        u = (a * _sigmoid(a)) * _dot(h, w3_ref[:, c0:c1])
        acc = acc + _dot(u.astype(BF16), w2_ref[c0:c1, :])
    h2 = _rms(acc, np_ref[...]).astype(BF16)
    gate = _sigmoid(_dot(h2, wpg_ref[...]))
    y = acc + gate * _dot(pe_ref[...].astype(BF16), wpe_ref[...])
    if final:
        y = _rms(y, fn_ref[...])
    o_ref[...] = y


def _ffn_ple(x, pe, lw, final_norm, layer, tm):
    n = x.shape[0]
    d_ff = lw["w_f1"].shape[1]
    half = d_ff // 2
    ff_chunks = ((0, half), (half, d_ff)) if half % LANES == 0 else ((0, d_ff),)
    final = final_norm is not None
    row = pl.BlockSpec((tm, D_MODEL), lambda i: (i, 0))
    in_specs = [row, pl.BlockSpec((None, tm, PLE_DIM), lambda i: (layer, i, 0)), _const_spec((1, D_MODEL)),
                _const_spec((D_MODEL, d_ff)), _const_spec((D_MODEL, d_ff)), _const_spec((d_ff, D_MODEL)),
                _const_spec((1, D_MODEL)), _const_spec((D_MODEL, D_MODEL)), _const_spec((PLE_DIM, D_MODEL))]
    args = [x, pe, lw["norm_ffn"], lw["w_f1"], lw["w_f3"], lw["w_f2"], lw["norm_ple"], lw["w_ple_gate"],
            lw["w_ple"]]
    if final:
        in_specs.append(_const_spec((1, D_MODEL)))
        args.append(final_norm)
    return pl.pallas_call(
        functools.partial(_ffn_ple_kernel, ff_chunks=ff_chunks, final=final), grid=(n // tm,),
        in_specs=in_specs, out_specs=row, out_shape=jax.ShapeDtypeStruct(x.shape, F32),
        compiler_params=_params(("parallel",)), name="ffn_ple",
    )(*args)


def _rope_tables(pos):
    half = ROPE_DIM // 2
    inv = ROPE_THETA ** (-jnp.arange(0, ROPE_DIM, 2, dtype=F32) / ROPE_DIM)
    ang = pos.astype(F32)[:, None] * inv[None, :]
    cos, sin = jnp.cos(ang), jnp.sin(ang)
    n = pos.shape[0]
    pad = jnp.zeros((n, A_DH - ROPE_DIM), F32)
    cos64 = jnp.concatenate([cos, cos, pad + 1.0], axis=1)
    s1_64 = jnp.concatenate([jnp.zeros((n, half), F32), sin, pad], axis=1)
    s2_64 = jnp.concatenate([-sin, jnp.zeros((n, half), F32), pad], axis=1)
    return tuple(jnp.concatenate([t, t], axis=1) for t in (cos64, s1_64, s2_64))


def _prep_weights(w_in, norm_mix, w_ga2, b_ga, w_pa, w_pb, w_o, norm_ffn, w_f1, w_f3, w_f2, norm_ple, w_ple,
                  w_ple_gate):
    depth = w_in.shape[0]
    a0 = 6144
    w_main = jnp.concatenate([w_in[:, :, :a0], w_in[:, :, a0 + G_RANK:]], axis=2).astype(BF16)
    w_ag = jnp.pad(w_in[:, :, a0:a0 + G_RANK], ((0, 0), (0, 0), (0, LANES - G_RANK))).astype(BF16)
    w_ga = jnp.pad(w_ga2, ((0, 0), (0, LANES - G_RANK), (0, 0))).astype(BF16)
    layers = []
    for l in range(depth):
        layers.append(dict(
            depth=depth,
            norm_mix=norm_mix[l][None], w_main=w_main[l], w_ag=w_ag[l], w_ga=w_ga[l], b_ga=b_ga[l][None],
            w_pa=w_pa[l].astype(BF16), w_pb=w_pb[l].astype(BF16), w_o=w_o[l].astype(BF16),
            norm_ffn=norm_ffn[l][None], w_f1=w_f1[l].astype(BF16), w_f3=w_f3[l].astype(BF16),
            w_f2=w_f2[l].astype(BF16), norm_ple=norm_ple[l][None], w_ple=w_ple[l].astype(BF16),
            w_ple_gate=w_ple_gate[l].astype(BF16)))
    return layers


def kernel(x_prompt, x_sample, p_prompt, p_sample, cache_k, cache_v, state_gla, w_in, norm_mix, lambda_q1,
           lambda_k1, lambda_q2, lambda_k2, a_subln, w_pa, w_ga2, b_ga, g_norm, w_pb, w_o, norm_ffn, w_f1, w_f3,
           w_f2, norm_ple, w_ple, w_ple_gate, final_norm):
    depth = w_in.shape[0]
    batch, seq, _ = x_prompt.shape
    dbatch, ts, _ = x_sample.shape
    past = cache_k.shape[2]
    n_p, n_s = batch * seq, dbatch * ts
    assert ts == CHUNK and seq % 512 == 0 and n_s % 8 == 0

    layers = _prep_weights(w_in, norm_mix, w_ga2, b_ga, w_pa, w_pb, w_o, norm_ffn, w_f1, w_f3, w_f2, norm_ple,
                           w_ple, w_ple_gate)
    rope_p = _rope_tables(jnp.arange(seq))
    rope_s = _rope_tables(jnp.tile(past + jnp.arange(ts), dbatch))
    lamp = jnp.pad(jnp.stack([lambda_q1, lambda_k1, lambda_q2, lambda_k2], axis=1),
                   ((0, 0), (0, 4), (0, LANES - A_DH)))
    fnorm = final_norm[None]
    tm_p = 512
    tm_s = min(512, n_s)
    pe_p = p_prompt.reshape(depth, n_p, PLE_DIM)
    pe_s = p_sample.reshape(depth, n_s, PLE_DIM)
    cache_k4 = cache_k.reshape(depth, dbatch, past, A_HEADS * A_DV)
    cache_v4 = cache_v.reshape(depth, dbatch, past, A_HEADS * A_DV)

    xp = x_prompt.reshape(n_p, D_MODEL)
    xs = x_sample.reshape(n_s, D_MODEL)
    kp = vp = sp = ks = vs = ss = None
    for l in range(depth):
        lw = layers[l]
        last = fnorm if l == depth - 1 else None
        sub = a_subln[l][None]
        gn = g_norm[l][None]

        q, kp, vp, qkg, vg, rg, g2, kT, gTh, gTl, sga, sgb = _in_proj(xp, lw, rope_p, l, kp, vp, tm_p)
        oa = _attn_seq(q, kp, vp, lamp[l], a_subln[l][:, None], l, batch, seq, 512, 2)
        og, sp = _gla(qkg, vg, rg, g2, kT, gTh, gTl, gn, None, sp, l, depth, batch, seq, 128, 512)
        xp = _merge(xp, oa, og, sga, sgb, lw, tm_p)
        xp = _ffn_ple(xp, pe_p, lw, last, l, tm_p)

        q, ks, vs, qkg, vg, rg, g2, kT, gTh, gTl, sga, sgb = _in_proj(xs, lw, rope_s, l, ks, vs, tm_s)
        oa = _attn_sample(q, cache_k4, cache_v4, ks, vs, lamp[l], sub, l, dbatch, ts)
        by_batch = lambda a: a.reshape(G_KW, dbatch, ts).transpose(1, 0, 2)
        og, ss = _gla(qkg, vg, rg, g2, by_batch(kT), by_batch(gTh), by_batch(gTl), gn, state_gla, ss, l, depth,
                      dbatch, ts, ts, ts)
        xs = _merge(xs, oa, og, sga, sgb, lw, tm_s)
        xs = _ffn_ple(xs, pe_s, lw, last, l, tm_s)

    shape_kv = lambda a, b, t: a.reshape(depth, b, t, A_HEADS, A_DV)
    return (xp.reshape(batch, seq, D_MODEL), xs.reshape(dbatch, ts, D_MODEL),
            shape_kv(kp, batch, seq), shape_kv(vp, batch, seq), sp,
            shape_kv(ks, dbatch, ts), shape_kv(vs, dbatch, ts), ss)
```

```python
import functools
import math

import numpy as np
import jax
import jax.numpy as jnp
from jax import lax
from jax.experimental import pallas as pl
from jax.experimental.pallas import tpu as pltpu

F32 = jnp.float32
BF16 = jnp.bfloat16

D_MODEL = 1024
CHUNK = 64
EPS = 1e-6
A_HEADS = 8
A_DH = 64
A_DV = 128
ROPE_DIM = 16
ROPE_THETA = 500000.0
G_HEADS = 4
G_DK = 128
G_DV = 256
G_KW = 512
G_RANK = 16
G_NORMALIZER = 16.0
PLE_DIM = 256
LANES = 128
ONES_ROWS = 16
VMEM_LIMIT = 56 * 1024 * 1024

Q_SCALE = A_DH ** -0.5 * math.log2(math.e)
G_SCALE = math.log2(math.e) / G_NORMALIZER
NT_DIMS = (((1,), (1,)), ((), ()))


def _dot(a, b):
    return jnp.dot(a, b, preferred_element_type=F32)


def _dot_nt(a, b):
    return lax.dot_general(a, b, NT_DIMS, preferred_element_type=F32)


def _sigmoid(x):
    return 1.0 / (1.0 + jnp.exp(-x))


def _log_sigmoid(x):
    return jnp.minimum(x, 0.0) - jnp.log(1.0 + jnp.exp(-jnp.abs(x)))


def _rms(x, w):
    return x * lax.rsqrt(jnp.mean(x * x, axis=-1, keepdims=True) + EPS) * w


def _const_spec(shape):
    nd = len(shape)
    return pl.BlockSpec(shape, lambda *_: (0,) * nd, pipeline_mode=pl.Buffered(1))


def _params(sem):
    return pltpu.CompilerParams(dimension_semantics=sem, vmem_limit_bytes=VMEM_LIMIT)


def _in_proj_kernel(*refs, aliased):
    x_ref, nw_ref, wm_ref, wag_ref, wga_ref, bga_ref, cos_ref, s1_ref, s2_ref = refs[:9]
    outs = refs[9 + (2 if aliased else 0):]
    (q_ref, k_ref, v_ref, qkg_ref, vg_ref, rg_ref, g2_ref, kT_ref, gTh_ref, gTl_ref,
     sga_ref, sgb_ref) = outs

    h = _rms(x_ref[...], nw_ref[...]).astype(BF16)

    def mm(c0, c1):
        return _dot(h, wm_ref[:, c0:c1])

    cos, s1, s2 = cos_ref[...], s1_ref[...], s2_ref[...]

    def rope_store(z, out_ref, scale):
        for c in range(z.shape[1] // LANES):
            zc = z[:, c * LANES:(c + 1) * LANES]
            r = zc * cos + pltpu.roll(zc, 8, 1) * s1 + pltpu.roll(zc, LANES - 8, 1) * s2
            if scale != 1.0:
                r = r * scale
            out_ref[:, c * LANES:(c + 1) * LANES] = r.astype(out_ref.dtype)

    rope_store(mm(0, 1024), q_ref, Q_SCALE)
    rope_store(mm(1024, 2048), k_ref, 1.0)
    v_ref[...] = mm(2048, 3072)
    z = mm(3072, 4096)
    qkg_ref[:, :G_KW] = (z[:, :G_KW] * (G_DK ** -0.5)).astype(BF16)
    qkg_ref[:, G_KW:] = z[:, G_KW:].astype(BF16)
    kT_ref[...] = z[:, G_KW:].T.astype(BF16)
    vg_ref[...] = mm(4096, 5120).astype(BF16)
    rg_ref[...] = mm(5120, 6144).astype(BF16)
    sga_ref[...] = _sigmoid(mm(6144, 7168)).astype(BF16)
    sgb_ref[...] = _sigmoid(mm(7168, 8192)).astype(BF16)

    ag = _dot(h, wag_ref[...]).astype(BF16)
    g = _log_sigmoid(_dot(ag, wga_ref[...]) + bga_ref[...]) * G_SCALE
    g_hi = g.astype(BF16)
    g2_ref[:, :G_KW] = g_hi
    g2_ref[:, G_KW:] = (g - g_hi.astype(F32)).astype(BF16)
    gT = g.T
    gT_hi = gT.astype(BF16)
    gTh_ref[...] = gT_hi
    gTl_ref[...] = (gT - gT_hi.astype(F32)).astype(BF16)


def _in_proj(x, lw, rope, layer, kbuf, vbuf, tm):
    n = x.shape[0]
    nt = n // tm
    cos, s1, s2 = rope
    nrb = cos.shape[0] // tm
    aliased = kbuf is not None
    depth = lw["depth"]

    row = lambda w: pl.BlockSpec((tm, w), lambda i: (i, 0))
    col = lambda h: pl.BlockSpec((h, tm), lambda i: (0, i))
    tab = pl.BlockSpec((tm, LANES), lambda i: (i % nrb, 0))
    in_specs = [row(D_MODEL), _const_spec((1, D_MODEL)), _const_spec((D_MODEL, 8192)),
                _const_spec((D_MODEL, LANES)), _const_spec((LANES, G_KW)), _const_spec((1, G_KW)), tab, tab, tab]
    args = [x, lw["norm_mix"], lw["w_main"], lw["w_ag"], lw["w_ga"], lw["b_ga"], cos, s1, s2]
    io_alias = {}
    if aliased:
        in_specs += [pl.BlockSpec(memory_space=pl.ANY)] * 2
        args += [kbuf, vbuf]
        io_alias = {9: 1, 10: 2}
    kv_spec = pl.BlockSpec((None, tm, D_MODEL), lambda i: (layer, i, 0))
    out_specs = [row(D_MODEL), kv_spec, kv_spec, row(1024), row(1024), row(1024), row(1024),
                 col(G_KW), col(G_KW), col(G_KW), row(D_MODEL), row(D_MODEL)]
    sd = jax.ShapeDtypeStruct
    out_shape = [sd((n, D_MODEL), BF16), sd((depth, n, D_MODEL), F32), sd((depth, n, D_MODEL), F32),
                 sd((n, 1024), BF16), sd((n, 1024), BF16), sd((n, 1024), BF16), sd((n, 1024), BF16),
                 sd((G_KW, n), BF16), sd((G_KW, n), BF16), sd((G_KW, n), BF16),
                 sd((n, D_MODEL), BF16), sd((n, D_MODEL), BF16)]
    return pl.pallas_call(
        functools.partial(_in_proj_kernel, aliased=aliased),
        grid=(nt,), in_specs=in_specs, out_specs=out_specs, out_shape=out_shape,
        input_output_aliases=io_alias, compiler_params=_params(("parallel",)), name="in_proj",
    )(*args)


def _lambda(lamp_ref, lam_init):
    lp = lamp_ref[...]
    d1 = jnp.sum(lp[0:1] * lp[1:2], axis=-1, keepdims=True)
    d2 = jnp.sum(lp[2:3] * lp[3:4], axis=-1, keepdims=True)
    return jnp.exp(d1) - jnp.exp(d2) + lam_init


def _attn_finish(o1, o2, lamp_ref, sub_ref, lam_init):
    o = o1 - _lambda(lamp_ref, lam_init) * o2
    return (_rms(o, sub_ref[...]) * (1.0 - lam_init)).astype(BF16)


def _attn_seq_kernel(q_ref, k_ref, v_ref, lamp_ref, subT_ref, bias_ref, o_ref, kb, vT, qT, m_scr, acc, s_scr,
                     mt_scr,
                     *, tq, nh, lam_init):
    seq = k_ref.shape[0]
    nq = seq // tq
    for h in range(nh):
        k = k_ref[:, h * A_DV:(h + 1) * A_DV]
        kb[h, 0] = k[:, :A_DH].astype(BF16)
        kb[h, 1] = k[:, A_DH:].astype(BF16)
        vT[h, 0:A_DV, :] = v_ref[:, h * A_DV:(h + 1) * A_DV].T.astype(BF16)
        vT[h, A_DV:, :] = jnp.ones((ONES_ROWS, seq), BF16)
        qt = q_ref[:, h * A_DV:(h + 1) * A_DV].astype(F32).T
        qT[h, 0] = qt[:A_DH].astype(BF16)
        qT[h, 1] = qt[A_DH:].astype(BF16)

    lam = _lambda(lamp_ref, lam_init)
    tiles = [(qi, t) for qi in range(nq) for t in range(qi + 1)]
    hq = tq // 2

    def scores(n):
        qi, t = tiles[n]
        slot = n % 2
        for h in range(nh):
            for mp in range(2):
                c = 2 * h + mp
                q_t = qT[h, mp, :, qi * tq:(qi + 1) * tq]
                if t == qi:
                    s_a = _dot(kb[h, mp, t * tq:t * tq + hq, :], q_t) + bias_ref[:, :tq]
                    s_b = _dot(kb[h, mp, t * tq + hq:(t + 1) * tq, :], q_t[:, hq:]) + bias_ref[:, tq:]
                    s_scr[slot, c, :hq, :] = s_a
                    s_scr[slot, c, hq:, hq:] = s_b
                    m_a = jnp.max(s_a, axis=0, keepdims=True)
                    mt_scr[slot, c] = jnp.concatenate(
                        [m_a[:, :hq], jnp.maximum(m_a[:, hq:], jnp.max(s_b, axis=0, keepdims=True))], axis=1)
                else:
                    s = _dot(kb[h, mp, t * tq:(t + 1) * tq, :], q_t)
                    s_scr[slot, c] = s
                    mt_scr[slot, c] = jnp.max(s, axis=0, keepdims=True)

    def softmax_pv(n):
        qi, t = tiles[n]
        slot = n % 2
        for h in range(nh):
            vt = vT[h, :, t * tq:(t + 1) * tq]
            for mp in range(2):
                c = 2 * h + mp
                m_tile = mt_scr[slot, c]
                m_new = m_tile if t == 0 else jnp.maximum(m_scr[c], m_tile)
                if t == qi:
                    p_a = jnp.exp2(s_scr[slot, c, :hq, :] - m_new).astype(BF16)
                    p_b = jnp.exp2(s_scr[slot, c, hq:, hq:] - m_new[:, hq:]).astype(BF16)
                    pv_a = _dot(vt[:, :hq], p_a)
                    pv = jnp.concatenate([pv_a[:, :hq], pv_a[:, hq:] + _dot(vt[:, hq:], p_b)], axis=1)
                else:
                    pv = _dot(vt, jnp.exp2(s_scr[slot, c] - m_new).astype(BF16))
                acc[c] = pv if t == 0 else jnp.exp2(m_scr[c] - m_new) * acc[c] + pv
                m_scr[c] = m_new

    def finish(qi):
        for h in range(nh):
            a1, a2 = acc[2 * h], acc[2 * h + 1]
            oT = a1[:A_DV] / a1[A_DV:A_DV + 1] - lam * (a2[:A_DV] / a2[A_DV:A_DV + 1])
            ms = jnp.mean(oT * oT, axis=0, keepdims=True)
            yT = oT * lax.rsqrt(ms + EPS) * subT_ref[...] * (1.0 - lam_init)
            o_ref[qi * tq:(qi + 1) * tq, h * A_DV:(h + 1) * A_DV] = yT.T.astype(BF16)

    scores(0)
    for n, (qi, t) in enumerate(tiles):
        if n + 1 < len(tiles):
            scores(n + 1)
        softmax_pv(n)
        if t == qi:
            finish(qi)


def _attn_seq(q, kbuf, vbuf, lamp, subT, layer, batch, seq, tq, nh):
    lam_init = 0.8 - 0.6 * math.exp(-0.3 * layer)
    kv_spec = pl.BlockSpec((None, seq, nh * A_DV), lambda b, h: (layer, b, h))
    qo_spec = pl.BlockSpec((seq, nh * A_DV), lambda b, h: (b, h))
    vm = pltpu.VMEM
    hq = tq // 2
    key_chunk = np.arange(hq)[:, None] // CHUNK
    qry_chunk = np.arange(tq)[None, :] // CHUNK
    visible = np.concatenate([key_chunk <= qry_chunk, key_chunk <= qry_chunk[:, :hq]], axis=1)
    bias = jnp.asarray(np.where(visible, 0.0, -np.inf), F32)
    return pl.pallas_call(
        functools.partial(_attn_seq_kernel, tq=tq, nh=nh, lam_init=lam_init),
        grid=(batch, A_HEADS // nh),
        in_specs=[qo_spec, kv_spec, kv_spec,
                  pl.BlockSpec((8, LANES), lambda b, h: (0, 0)), pl.BlockSpec((A_DV, 1), lambda b, h: (0, 0)),
                  _const_spec(bias.shape)],
        out_specs=qo_spec,
        out_shape=jax.ShapeDtypeStruct(q.shape, BF16),
        scratch_shapes=[vm((nh, 2, seq, A_DH), BF16), vm((nh, A_DV + ONES_ROWS, seq), BF16),
                        vm((nh, 2, A_DH, seq), BF16), vm((2 * nh, 1, tq), F32),
                        vm((2 * nh, A_DV + ONES_ROWS, tq), F32), vm((2, 2 * nh, tq, tq), F32),
                        vm((2, 2 * nh, 1, tq), F32)],
        compiler_params=_params(("parallel", "parallel")), name="attn_prompt",
    )(q, kbuf, vbuf, lamp, subT, bias)


def _attn_sample_kernel(q_ref, kc_ref, vc_ref, kn_ref, vn_ref, lamp_ref, sub_ref, o_ref, *, lam_init):
    q = q_ref[...]
    kc = kc_ref[...]
    vc = vc_ref[...]
    kn = kn_ref[...].astype(BF16)
    vn = vn_ref[...].astype(BF16)
    outs = []
    for lo in (0, A_DH):
        qq = q[:, lo:lo + A_DH]
        sc = _dot_nt(qq, kc[:, lo:lo + A_DH])
        sn = _dot_nt(qq, kn[:, lo:lo + A_DH])
        m = jnp.maximum(jnp.max(sc, axis=-1, keepdims=True), jnp.max(sn, axis=-1, keepdims=True))
        pc = jnp.exp2(sc - m)
        pn = jnp.exp2(sn - m)
        denom = jnp.sum(pc, axis=-1, keepdims=True) + jnp.sum(pn, axis=-1, keepdims=True)
        outs.append((_dot(pc.astype(BF16), vc) + _dot(pn.astype(BF16), vn)) / denom)
    o_ref[...] = _attn_finish(outs[0], outs[1], lamp_ref, sub_ref, lam_init)


def _attn_sample(q, cache_k, cache_v, kbuf, vbuf, lamp, sub, layer, batch, ts):
    past = cache_k.shape[2]
    lam_init = 0.8 - 0.6 * math.exp(-0.3 * layer)
    cache_spec = pl.BlockSpec((None, None, past, A_DV), lambda b, h: (layer, b, 0, h))
    new_spec = pl.BlockSpec((None, ts, A_DV), lambda b, h: (layer, b, h))
    qo_spec = pl.BlockSpec((ts, A_DV), lambda b, h: (b, h))
    return pl.pallas_call(
        functools.partial(_attn_sample_kernel, lam_init=lam_init),
        grid=(batch, A_HEADS),
        in_specs=[qo_spec, cache_spec, cache_spec, new_spec, new_spec,
                  pl.BlockSpec((8, LANES), lambda b, h: (0, 0)), pl.BlockSpec((1, A_DV), lambda b, h: (0, 0))],
        out_specs=qo_spec,
        out_shape=jax.ShapeDtypeStruct(q.shape, BF16),
        compiler_params=_params(("parallel", "parallel")), name="attn_sample",
    )(q, cache_k, cache_v, kbuf, vbuf, lamp, sub)


def _gla_tables(c):
    levels = int(math.log2(c))
    t = np.arange(c)[:, None]
    r = np.arange(c)[None, :]
    blocks = [(r <= t)]
    for p in range(levels):
        s = 1 << p
        m = (t >> (p + 1) << (p + 1)) + s - 1
        upper = ((t >> p) & 1) == 1
        blocks.append(np.where(upper, (r > m) & (r <= t), (r > t) & (r <= m)))
    l_rows = np.concatenate(blocks, axis=0).astype(np.float32)
    tail_t = (np.arange(c)[:, None] > np.arange(c)[None, :]).astype(np.float32)
    rt = np.concatenate([tail_t, np.ones((c, G_DV), np.float32)], axis=1)
    i, j = t, r
    masks = [i == j] + [(((i ^ j) >> p) == 1) & (((i >> p) & 1) == 1) for p in range(levels)]
    return (jnp.asarray(np.concatenate([l_rows, l_rows], axis=1), BF16),
            jnp.asarray(np.concatenate([rt, rt], axis=0), BF16),
            jnp.asarray(np.stack(masks).astype(np.float32)), levels)


def _gla_kernel(*refs, c, nch, levels, has_s0, aliased):
    qk_ref, v_ref, r_ref, g2_ref, kT_ref, gTh_ref, gTl_ref, l2_ref, rt_ref, pm_ref, gn_ref = refs[:11]
    pos = 11
    s0_ref = None
    if has_s0:
        s0_ref = refs[pos]
        pos += 1
    if aliased:
        pos += 1
    o_ref, sout_ref, s_scr = refs[pos:pos + 3]
    t = pl.program_id(1)

    @pl.when(t == 0)
    def _():
        if has_s0:
            s_scr[...] = s0_ref[...]
        else:
            s_scr[...] = jnp.zeros(s_scr.shape, F32)

    gn = gn_ref[...]

    def chunk(r0):
        rows = pl.ds(r0, c)
        g2 = g2_ref[rows, :]
        x_all = _dot(l2_ref[...], jnp.concatenate([g2[:, :G_KW], g2[:, G_KW:]], axis=0))
        gT = jnp.concatenate([gTh_ref[:, rows], gTl_ref[:, rows]], axis=1)
        xT = _dot(gT, rt_ref[...])
        for h in range(G_HEADS):
            dk = slice(h * G_DK, (h + 1) * G_DK)
            dv = slice(h * G_DV, (h + 1) * G_DV)
            qb = qk_ref[rows, dk]
            kb = qk_ref[rows, G_KW + h * G_DK:G_KW + (h + 1) * G_DK]
            q = qb.astype(F32)
            k = kb.astype(F32)
            v = v_ref[rows, dv]
            a = pm_ref[0] * _dot_nt(qb, kb)
            for p in range(levels):
                e = jnp.exp2(x_all[(1 + p) * c:(2 + p) * c, dk])
                a = a + pm_ref[1 + p] * _dot_nt((q * e).astype(BF16), (k * e).astype(BF16))
            s_old = s_scr[h]
            qe = (q * jnp.exp2(x_all[0:c, dk])).astype(BF16)
            o = _dot(qe, s_old.astype(BF16)) + _dot(a.astype(BF16), v)
            rg = r_ref[rows, dv].astype(F32)
            o_ref[rows, dv] = (_rms(o, gn) * (rg * _sigmoid(rg))).astype(BF16)
            kT = kT_ref[dk, rows].astype(F32)
            kdT = (kT * jnp.exp2(xT[dk, 0:c])).astype(BF16)
            s_scr[h] = jnp.exp2(xT[dk, c:c + G_DV]) * s_old + _dot(kdT, v)

    for i in range(nch):
        chunk(i * c)

    @pl.when(t == pl.num_programs(1) - 1)
    def _():
        sout_ref[...] = s_scr[...]


def _gla(qkg, vg, rg, g2, kT, gTh, gTl, gn, s0, sbuf, layer, depth, batch, seq, c, cg):
    n = batch * seq
    nb = seq // cg
    l2, rt, pmask, levels = _gla_tables(c)
    has_s0 = s0 is not None
    aliased = sbuf is not None
    row = pl.BlockSpec((cg, 1024), lambda b, t: (b * nb + t, 0))
    if kT.ndim == 3:
        col = pl.BlockSpec((None, G_KW, cg), lambda b, t: (b, 0, t))
    else:
        col = pl.BlockSpec((G_KW, cg), lambda b, t: (0, b * nb + t))
    in_specs = [row, row, row, row, col, col, col, _const_spec(l2.shape), _const_spec(rt.shape),
                _const_spec(pmask.shape), _const_spec((1, G_DV))]
    args = [qkg, vg, rg, g2, kT, gTh, gTl, l2, rt, pmask, gn]
    if has_s0:
        in_specs.append(pl.BlockSpec((None, None, G_HEADS, G_DK, G_DV), lambda b, t: (layer, b, 0, 0, 0)))
        args.append(s0)
    io_alias = {}
    if aliased:
        in_specs.append(pl.BlockSpec(memory_space=pl.ANY))
        args.append(sbuf)
        io_alias = {len(args) - 1: 1}
    s_spec = pl.BlockSpec((None, None, G_HEADS, G_DK, G_DV), lambda b, t: (layer, b, 0, 0, 0))
    return pl.pallas_call(
        functools.partial(_gla_kernel, c=c, nch=cg // c, levels=levels, has_s0=has_s0, aliased=aliased),
        grid=(batch, nb), in_specs=in_specs, out_specs=[row, s_spec],
        out_shape=[jax.ShapeDtypeStruct((n, 1024), BF16),
                   jax.ShapeDtypeStruct((depth, batch, G_HEADS, G_DK, G_DV), F32)],
        scratch_shapes=[pltpu.VMEM((G_HEADS, G_DK, G_DV), F32)],
        input_output_aliases=io_alias,
        compiler_params=_params(("parallel", "arbitrary")), name="gla",
    )(*args)


def _merge_kernel(x_ref, oa_ref, og_ref, sga_ref, sgb_ref, wpa_ref, wpb_ref, wo_ref, o_ref):
    ua = _dot(oa_ref[...], wpa_ref[...])
    ub = _dot(og_ref[...], wpb_ref[...])
    mixed = sga_ref[...].astype(F32) * ua + sgb_ref[...].astype(F32) * ub
    o_ref[...] = x_ref[...] + _dot(mixed.astype(BF16), wo_ref[...])


def _merge(x, oa, og, sga, sgb, lw, tm):
    n = x.shape[0]
    row = pl.BlockSpec((tm, D_MODEL), lambda i: (i, 0))
    wspec = _const_spec((D_MODEL, D_MODEL))
    return pl.pallas_call(
        _merge_kernel, grid=(n // tm,),
        in_specs=[row, row, row, row, row, wspec, wspec, wspec], out_specs=row,
        out_shape=jax.ShapeDtypeStruct(x.shape, F32),
        compiler_params=_params(("parallel",)), name="merge",
    )(x, oa, og, sga, sgb, lw["w_pa"], lw["w_pb"], lw["w_o"])


def _ffn_ple_kernel(*refs, ff_chunks, final):
    (x_ref, pe_ref, nf_ref, w1_ref, w3_ref, w2_ref, np_ref, wpg_ref, wpe_ref) = refs[:9]
    fn_ref = refs[9] if final else None
    o_ref = refs[-1]
    x = x_ref[...]
    h = _rms(x, nf_ref[...]).astype(BF16)
    acc = x
    for c0, c1 in ff_chunks:
        a = _dot(h, w1_ref[:, c0:c1])
        u = (a * _sigmoid(a)) * _dot(h, w3_ref[:, c0:c1])
        acc = acc + _dot(u.astype(BF16), w2_ref[c0:c1, :])
    h2 = _rms(acc, np_ref[...]).astype(BF16)
    gate = _sigmoid(_dot(h2, wpg_ref[...]))
    y = acc + gate * _dot(pe_ref[...].astype(BF16), wpe_ref[...])
    if final:
        y = _rms(y, fn_ref[...])
    o_ref[...] = y


def _ffn_ple(x, pe, lw, final_norm, layer, tm):
    n = x.shape[0]
    d_ff = lw["w_f1"].shape[1]
    half = d_ff // 2
    ff_chunks = ((0, half), (half, d_ff)) if half % LANES == 0 else ((0, d_ff),)
    final = final_norm is not None
    row = pl.BlockSpec((tm, D_MODEL), lambda i: (i, 0))
    in_specs = [row, pl.BlockSpec((None, tm, PLE_DIM), lambda i: (layer, i, 0)), _const_spec((1, D_MODEL)),
                _const_spec((D_MODEL, d_ff)), _const_spec((D_MODEL, d_ff)), _const_spec((d_ff, D_MODEL)),
                _const_spec((1, D_MODEL)), _const_spec((D_MODEL, D_MODEL)), _const_spec((PLE_DIM, D_MODEL))]
    args = [x, pe, lw["norm_ffn"], lw["w_f1"], lw["w_f3"], lw["w_f2"], lw["norm_ple"], lw["w_ple_gate"],
            lw["w_ple"]]
    if final:
        in_specs.append(_const_spec((1, D_MODEL)))
        args.append(final_norm)
    return pl.pallas_call(
        functools.partial(_ffn_ple_kernel, ff_chunks=ff_chunks, final=final), grid=(n // tm,),
        in_specs=in_specs, out_specs=row, out_shape=jax.ShapeDtypeStruct(x.shape, F32),
        compiler_params=_params(("parallel",)), name="ffn_ple",
    )(*args)


def _rope_tables(pos):
    half = ROPE_DIM // 2
    inv = ROPE_THETA ** (-jnp.arange(0, ROPE_DIM, 2, dtype=F32) / ROPE_DIM)
    ang = pos.astype(F32)[:, None] * inv[None, :]
    cos, sin = jnp.cos(ang), jnp.sin(ang)
    n = pos.shape[0]
    pad = jnp.zeros((n, A_DH - ROPE_DIM), F32)
    cos64 = jnp.concatenate([cos, cos, pad + 1.0], axis=1)
    s1_64 = jnp.concatenate([jnp.zeros((n, half), F32), sin, pad], axis=1)
    s2_64 = jnp.concatenate([-sin, jnp.zeros((n, half), F32), pad], axis=1)
    return tuple(jnp.concatenate([t, t], axis=1) for t in (cos64, s1_64, s2_64))


def _prep_weights(w_in, norm_mix, w_ga2, b_ga, w_pa, w_pb, w_o, norm_ffn, w_f1, w_f3, w_f2, norm_ple, w_ple,
                  w_ple_gate):
    depth = w_in.shape[0]
    a0 = 6144
    w_main = jnp.concatenate([w_in[:, :, :a0], w_in[:, :, a0 + G_RANK:]], axis=2).astype(BF16)
    w_ag = jnp.pad(w_in[:, :, a0:a0 + G_RANK], ((0, 0), (0, 0), (0, LANES - G_RANK))).astype(BF16)
    w_ga = jnp.pad(w_ga2, ((0, 0), (0, LANES - G_RANK), (0, 0))).astype(BF16)
    layers = []
    for l in range(depth):
        layers.append(dict(
            depth=depth,
            norm_mix=norm_mix[l][None], w_main=w_main[l], w_ag=w_ag[l], w_ga=w_ga[l], b_ga=b_ga[l][None],
            w_pa=w_pa[l].astype(BF16), w_pb=w_pb[l].astype(BF16), w_o=w_o[l].astype(BF16),
            norm_ffn=norm_ffn[l][None], w_f1=w_f1[l].astype(BF16), w_f3=w_f3[l].astype(BF16),
            w_f2=w_f2[l].astype(BF16), norm_ple=norm_ple[l][None], w_ple=w_ple[l].astype(BF16),
            w_ple_gate=w_ple_gate[l].astype(BF16)))
    return layers


def kernel(x_prompt, x_sample, p_prompt, p_sample, cache_k, cache_v, state_gla, w_in, norm_mix, lambda_q1,
           lambda_k1, lambda_q2, lambda_k2, a_subln, w_pa, w_ga2, b_ga, g_norm, w_pb, w_o, norm_ffn, w_f1, w_f3,
           w_f2, norm_ple, w_ple, w_ple_gate, final_norm):
    depth = w_in.shape[0]
    batch, seq, _ = x_prompt.shape
    dbatch, ts, _ = x_sample.shape
    past = cache_k.shape[2]
    n_p, n_s = batch * seq, dbatch * ts
    assert ts == CHUNK and seq % 512 == 0 and n_s % 8 == 0

    layers = _prep_weights(w_in, norm_mix, w_ga2, b_ga, w_pa, w_pb, w_o, norm_ffn, w_f1, w_f3, w_f2, norm_ple,
                           w_ple, w_ple_gate)
    rope_p = _rope_tables(jnp.arange(seq))
    rope_s = _rope_tables(jnp.tile(past + jnp.arange(ts), dbatch))
    lamp = jnp.pad(jnp.stack([lambda_q1, lambda_k1, lambda_q2, lambda_k2], axis=1),
                   ((0, 0), (0, 4), (0, LANES - A_DH)))
    fnorm = final_norm[None]
    tm_p = 512
    tm_s = min(512, n_s)
    pe_p = p_prompt.reshape(depth, n_p, PLE_DIM)
    pe_s = p_sample.reshape(depth, n_s, PLE_DIM)
    cache_k4 = cache_k.astype(BF16).reshape(depth, dbatch, past, A_HEADS * A_DV)
    cache_v4 = cache_v.astype(BF16).reshape(depth, dbatch, past, A_HEADS * A_DV)

    xp = x_prompt.reshape(n_p, D_MODEL)
    xs = x_sample.reshape(n_s, D_MODEL)
    kp = vp = sp = ks = vs = ss = None
    for l in range(depth):
        lw = layers[l]
        last = fnorm if l == depth - 1 else None
        sub = a_subln[l][None]
        gn = g_norm[l][None]

        q, kp, vp, qkg, vg, rg, g2, kT, gTh, gTl, sga, sgb = _in_proj(xp, lw, rope_p, l, kp, vp, tm_p)
        oa = _attn_seq(q, kp, vp, lamp[l], a_subln[l][:, None], l, batch, seq, 512, 2)
        og, sp = _gla(qkg, vg, rg, g2, kT, gTh, gTl, gn, None, sp, l, depth, batch, seq, 128, 512)
        xp = _merge(xp, oa, og, sga, sgb, lw, tm_p)
        xp = _ffn_ple(xp, pe_p, lw, last, l, tm_p)

        q, ks, vs, qkg, vg, rg, g2, kT, gTh, gTl, sga, sgb = _in_proj(xs, lw, rope_s, l, ks, vs, tm_s)
        oa = _attn_sample(q, cache_k4, cache_v4, ks, vs, lamp[l], sub, l, dbatch, ts)
        by_batch = lambda a: a.reshape(G_KW, dbatch, ts).transpose(1, 0, 2)
        og, ss = _gla(qkg, vg, rg, g2, by_batch(kT), by_batch(gTh), by_batch(gTl), gn, state_gla, ss, l, depth,
                      dbatch, ts, ts, ts)
        xs = _merge(xs, oa, og, sga, sgb, lw, tm_s)
        xs = _ffn_ple(xs, pe_s, lw, last, l, tm_s)

    shape_kv = lambda a, b, t: a.reshape(depth, b, t, A_HEADS, A_DV)
    return (xp.reshape(batch, seq, D_MODEL), xs.reshape(dbatch, ts, D_MODEL),
            shape_kv(kp, batch, seq), shape_kv(vp, batch, seq), sp,
            shape_kv(ks, dbatch, ts), shape_kv(vs, dbatch, ts), ss)
```

```python
import functools
import math

import numpy as np
import jax
import jax.numpy as jnp
from jax import lax
from jax.experimental import pallas as pl
from jax.experimental.pallas import tpu as pltpu

F32 = jnp.float32
BF16 = jnp.bfloat16

D_MODEL = 1024
CHUNK = 64
EPS = 1e-6
A_HEADS = 8
A_DH = 64
A_DV = 128
ROPE_DIM = 16
ROPE_THETA = 500000.0
G_HEADS = 4
G_DK = 128
G_DV = 256
G_KW = 512
G_RANK = 16
G_NORMALIZER = 16.0
PLE_DIM = 256
LANES = 128
ONES_ROWS = 16
VMEM_LIMIT = 56 * 1024 * 1024

COL_QA, COL_KA, COL_VA, COL_QKG, COL_VG, COL_RG, COL_GA, COL_GB, W_MAIN = (1024 * i for i in range(9))
COL_AG = COL_GA

TOKEN_TILE = 512
ATTN_TILE = 512
ATTN_HEADS_PER_STEP = 2
GLA_CHUNK = 128
GLA_BLOCK = 512

Q_SCALE = A_DH ** -0.5 * math.log2(math.e)
G_SCALE = math.log2(math.e) / G_NORMALIZER
NT_DIMS = (((1,), (1,)), ((), ()))


def _dot(a, b):
    return jnp.dot(a, b, preferred_element_type=F32)


def _dot_nt(a, b):
    return lax.dot_general(a, b, NT_DIMS, preferred_element_type=F32)


def _sigmoid(x):
    return 1.0 / (1.0 + jnp.exp(-x))


def _log_sigmoid(x):
    return jnp.minimum(x, 0.0) - jnp.log(1.0 + jnp.exp(-jnp.abs(x)))


def _rms(x, w):
    return x * lax.rsqrt(jnp.mean(x * x, axis=-1, keepdims=True) + EPS) * w


def _const_spec(shape):
    nd = len(shape)
    return pl.BlockSpec(shape, lambda *_: (0,) * nd, pipeline_mode=pl.Buffered(1))


def _params(sem):
    return pltpu.CompilerParams(dimension_semantics=sem, vmem_limit_bytes=VMEM_LIMIT)


def _in_proj_kernel(*refs, aliased):
    x_ref, nw_ref, wm_ref, wag_ref, wga_ref, bga_ref, cos_ref, s1_ref, s2_ref = refs[:9]
    outs = refs[9 + (2 if aliased else 0):]
    (q_ref, k_ref, v_ref, qkg_ref, vg_ref, rg_ref, g2_ref, kT_ref, gTh_ref, gTl_ref,
     sga_ref, sgb_ref) = outs

    h = _rms(x_ref[...], nw_ref[...]).astype(BF16)

    def mm(c0, c1):
        return _dot(h, wm_ref[:, c0:c1])

    cos, s1, s2 = cos_ref[...], s1_ref[...], s2_ref[...]

    def rope_store(z, out_ref, scale):
        for c in range(z.shape[1] // LANES):
            zc = z[:, c * LANES:(c + 1) * LANES]
            r = zc * cos + pltpu.roll(zc, 8, 1) * s1 + pltpu.roll(zc, LANES - 8, 1) * s2
            if scale != 1.0:
                r = r * scale
            out_ref[:, c * LANES:(c + 1) * LANES] = r.astype(out_ref.dtype)

    rope_store(mm(COL_QA, COL_KA), q_ref, Q_SCALE)
    rope_store(mm(COL_KA, COL_VA), k_ref, 1.0)
    v_ref[...] = mm(COL_VA, COL_QKG)
    z = mm(COL_QKG, COL_VG)
    qkg_ref[:, :G_KW] = (z[:, :G_KW] * (G_DK ** -0.5)).astype(BF16)
    qkg_ref[:, G_KW:] = z[:, G_KW:].astype(BF16)
    kT_ref[...] = z[:, G_KW:].T.astype(BF16)
    vg_ref[...] = mm(COL_VG, COL_RG).astype(BF16)
    rg_ref[...] = mm(COL_RG, COL_GA).astype(BF16)
    sga_ref[...] = _sigmoid(mm(COL_GA, COL_GB)).astype(BF16)
    sgb_ref[...] = _sigmoid(mm(COL_GB, W_MAIN)).astype(BF16)

    ag = _dot(h, wag_ref[...]).astype(BF16)
    g = _log_sigmoid(_dot(ag, wga_ref[...]) + bga_ref[...]) * G_SCALE
    g_hi = g.astype(BF16)
    g2_ref[:, :G_KW] = g_hi
    g2_ref[:, G_KW:] = (g - g_hi.astype(F32)).astype(BF16)
    gT = g.T
    gT_hi = gT.astype(BF16)
    gTh_ref[...] = gT_hi
    gTl_ref[...] = (gT - gT_hi.astype(F32)).astype(BF16)


def _in_proj(x, lw, rope, layer, kbuf, vbuf, tm):
    n = x.shape[0]
    nt = n // tm
    cos, s1, s2 = rope
    nrb = cos.shape[0] // tm
    aliased = kbuf is not None
    depth = lw["depth"]

    row = lambda w: pl.BlockSpec((tm, w), lambda i: (i, 0))
    col = lambda h: pl.BlockSpec((h, tm), lambda i: (0, i))
    tab = pl.BlockSpec((tm, LANES), lambda i: (i % nrb, 0))
    in_specs = [row(D_MODEL), _const_spec((1, D_MODEL)), _const_spec((D_MODEL, W_MAIN)),
                _const_spec((D_MODEL, LANES)), _const_spec((LANES, G_KW)), _const_spec((1, G_KW)), tab, tab, tab]
    args = [x, lw["norm_mix"], lw["w_main"], lw["w_ag"], lw["w_ga"], lw["b_ga"], cos, s1, s2]
    io_alias = {}
    if aliased:
        in_specs += [pl.BlockSpec(memory_space=pl.ANY)] * 2
        args += [kbuf, vbuf]
        io_alias = {9: 1, 10: 2}
    kv_spec = pl.BlockSpec((None, tm, D_MODEL), lambda i: (layer, i, 0))
    out_specs = [row(D_MODEL), kv_spec, kv_spec, row(1024), row(1024), row(1024), row(1024),
                 col(G_KW), col(G_KW), col(G_KW), row(D_MODEL), row(D_MODEL)]
    sd = jax.ShapeDtypeStruct
    out_shape = [sd((n, D_MODEL), BF16), sd((depth, n, D_MODEL), F32), sd((depth, n, D_MODEL), F32),
                 sd((n, 1024), BF16), sd((n, 1024), BF16), sd((n, 1024), BF16), sd((n, 1024), BF16),
                 sd((G_KW, n), BF16), sd((G_KW, n), BF16), sd((G_KW, n), BF16),
                 sd((n, D_MODEL), BF16), sd((n, D_MODEL), BF16)]
    return pl.pallas_call(
        functools.partial(_in_proj_kernel, aliased=aliased),
        grid=(nt,), in_specs=in_specs, out_specs=out_specs, out_shape=out_shape,
        input_output_aliases=io_alias, compiler_params=_params(("parallel",)), name="in_proj",
    )(*args)


def _lambda(lamp_ref, lam_init):
    lp = lamp_ref[...]
    d1 = jnp.sum(lp[0:1] * lp[1:2], axis=-1, keepdims=True)
    d2 = jnp.sum(lp[2:3] * lp[3:4], axis=-1, keepdims=True)
    return jnp.exp(d1) - jnp.exp(d2) + lam_init


def _attn_finish(o1, o2, lamp_ref, sub_ref, lam_init):
    o = o1 - _lambda(lamp_ref, lam_init) * o2
    return (_rms(o, sub_ref[...]) * (1.0 - lam_init)).astype(BF16)


def _attn_seq_kernel(q_ref, k_ref, v_ref, lamp_ref, subT_ref, bias_ref, o_ref, kb, vT, qT, m_scr, acc, s_scr,
                     mt_scr,
                     *, tq, nh, lam_init):
    seq = k_ref.shape[0]
    nq = seq // tq
    for h in range(nh):
        k = k_ref[:, h * A_DV:(h + 1) * A_DV]
        kb[h, 0] = k[:, :A_DH].astype(BF16)
        kb[h, 1] = k[:, A_DH:].astype(BF16)
        vT[h, 0:A_DV, :] = v_ref[:, h * A_DV:(h + 1) * A_DV].T.astype(BF16)
        vT[h, A_DV:, :] = jnp.ones((ONES_ROWS, seq), BF16)
        qt = q_ref[:, h * A_DV:(h + 1) * A_DV].astype(F32).T
        qT[h, 0] = qt[:A_DH].astype(BF16)
        qT[h, 1] = qt[A_DH:].astype(BF16)

    lam = _lambda(lamp_ref, lam_init)
    tiles = [(qi, t) for qi in range(nq) for t in range(qi + 1)]
    hq = tq // 2

    def scores(n):
        qi, t = tiles[n]
        slot = n % 2
        for h in range(nh):
            for mp in range(2):
                c = 2 * h + mp
                q_t = qT[h, mp, :, qi * tq:(qi + 1) * tq]
                if t == qi:
                    s_a = _dot(kb[h, mp, t * tq:t * tq + hq, :], q_t) + bias_ref[:, :tq]
                    s_b = _dot(kb[h, mp, t * tq + hq:(t + 1) * tq, :], q_t[:, hq:]) + bias_ref[:, tq:]
                    s_scr[slot, c, :hq, :] = s_a
                    s_scr[slot, c, hq:, hq:] = s_b
                    m_a = jnp.max(s_a, axis=0, keepdims=True)
                    mt_scr[slot, c] = jnp.concatenate(
                        [m_a[:, :hq], jnp.maximum(m_a[:, hq:], jnp.max(s_b, axis=0, keepdims=True))], axis=1)
                else:
                    s = _dot(kb[h, mp, t * tq:(t + 1) * tq, :], q_t)
                    s_scr[slot, c] = s
                    mt_scr[slot, c] = jnp.max(s, axis=0, keepdims=True)

    def softmax_pv(n):
        qi, t = tiles[n]
        slot = n % 2
        for h in range(nh):
            vt = vT[h, :, t * tq:(t + 1) * tq]
            for mp in range(2):
                c = 2 * h + mp
                m_tile = mt_scr[slot, c]
                m_new = m_tile if t == 0 else jnp.maximum(m_scr[c], m_tile)
                if t == qi:
                    p_a = jnp.exp2(s_scr[slot, c, :hq, :] - m_new).astype(BF16)
                    p_b = jnp.exp2(s_scr[slot, c, hq:, hq:] - m_new[:, hq:]).astype(BF16)
                    pv_a = _dot(vt[:, :hq], p_a)
                    pv = jnp.concatenate([pv_a[:, :hq], pv_a[:, hq:] + _dot(vt[:, hq:], p_b)], axis=1)
                else:
                    pv = _dot(vt, jnp.exp2(s_scr[slot, c] - m_new).astype(BF16))
                acc[c] = pv if t == 0 else jnp.exp2(m_scr[c] - m_new) * acc[c] + pv
                m_scr[c] = m_new

    def finish(qi):
        for h in range(nh):
            a1, a2 = acc[2 * h], acc[2 * h + 1]
            oT = a1[:A_DV] / a1[A_DV:A_DV + 1] - lam * (a2[:A_DV] / a2[A_DV:A_DV + 1])
            ms = jnp.mean(oT * oT, axis=0, keepdims=True)
            yT = oT * lax.rsqrt(ms + EPS) * subT_ref[...] * (1.0 - lam_init)
            o_ref[qi * tq:(qi + 1) * tq, h * A_DV:(h + 1) * A_DV] = yT.T.astype(BF16)

    scores(0)
    for n, (qi, t) in enumerate(tiles):
        if n + 1 < len(tiles):
            scores(n + 1)
        softmax_pv(n)
        if t == qi:
            finish(qi)


def _attn_seq(q, kbuf, vbuf, lamp, subT, layer, batch, seq, tq, nh):
    lam_init = 0.8 - 0.6 * math.exp(-0.3 * layer)
    kv_spec = pl.BlockSpec((None, seq, nh * A_DV), lambda b, h: (layer, b, h))
    qo_spec = pl.BlockSpec((seq, nh * A_DV), lambda b, h: (b, h))
    vm = pltpu.VMEM
    hq = tq // 2
    key_chunk = np.arange(hq)[:, None] // CHUNK
    qry_chunk = np.arange(tq)[None, :] // CHUNK
    visible = np.concatenate([key_chunk <= qry_chunk, key_chunk <= qry_chunk[:, :hq]], axis=1)
    bias = jnp.asarray(np.where(visible, 0.0, -np.inf), F32)
    return pl.pallas_call(
        functools.partial(_attn_seq_kernel, tq=tq, nh=nh, lam_init=lam_init),
        grid=(batch, A_HEADS // nh),
        in_specs=[qo_spec, kv_spec, kv_spec,
                  pl.BlockSpec((8, LANES), lambda b, h: (0, 0)), pl.BlockSpec((A_DV, 1), lambda b, h: (0, 0)),
                  _const_spec(bias.shape)],
        out_specs=qo_spec,
        out_shape=jax.ShapeDtypeStruct(q.shape, BF16),
        scratch_shapes=[vm((nh, 2, seq, A_DH), BF16), vm((nh, A_DV + ONES_ROWS, seq), BF16),
                        vm((nh, 2, A_DH, seq), BF16), vm((2 * nh, 1, tq), F32),
                        vm((2 * nh, A_DV + ONES_ROWS, tq), F32), vm((2, 2 * nh, tq, tq), F32),
                        vm((2, 2 * nh, 1, tq), F32)],
        compiler_params=_params(("parallel", "parallel")), name="attn_prompt",
    )(q, kbuf, vbuf, lamp, subT, bias)


def _attn_sample_kernel(q_ref, kc_ref, vc_ref, kn_ref, vn_ref, lamp_ref, sub_ref, o_ref, *, lam_init):
    q = q_ref[...]
    kc = kc_ref[...].astype(BF16)
    vc = vc_ref[...].astype(BF16)
    kn = kn_ref[...].astype(BF16)
    vn = vn_ref[...].astype(BF16)
    outs = []
    for lo in (0, A_DH):
        qq = q[:, lo:lo + A_DH]
        sc = _dot_nt(qq, kc[:, lo:lo + A_DH])
        sn = _dot_nt(qq, kn[:, lo:lo + A_DH])
        m = jnp.maximum(jnp.max(sc, axis=-1, keepdims=True), jnp.max(sn, axis=-1, keepdims=True))
        pc = jnp.exp2(sc - m)
        pn = jnp.exp2(sn - m)
        denom = jnp.sum(pc, axis=-1, keepdims=True) + jnp.sum(pn, axis=-1, keepdims=True)
        outs.append((_dot(pc.astype(BF16), vc) + _dot(pn.astype(BF16), vn)) / denom)
    o_ref[...] = _attn_finish(outs[0], outs[1], lamp_ref, sub_ref, lam_init)


def _attn_sample(q, cache_k, cache_v, kbuf, vbuf, lamp, sub, layer, batch, ts):
    past = cache_k.shape[2]
    lam_init = 0.8 - 0.6 * math.exp(-0.3 * layer)
    cache_spec = pl.BlockSpec((None, None, past, A_DV), lambda b, h: (layer, b, 0, h))
    new_spec = pl.BlockSpec((None, ts, A_DV), lambda b, h: (layer, b, h))
    qo_spec = pl.BlockSpec((ts, A_DV), lambda b, h: (b, h))
    return pl.pallas_call(
        functools.partial(_attn_sample_kernel, lam_init=lam_init),
        grid=(batch, A_HEADS),
        in_specs=[qo_spec, cache_spec, cache_spec, new_spec, new_spec,
                  pl.BlockSpec((8, LANES), lambda b, h: (0, 0)), pl.BlockSpec((1, A_DV), lambda b, h: (0, 0))],
        out_specs=qo_spec,
        out_shape=jax.ShapeDtypeStruct(q.shape, BF16),
        compiler_params=_params(("parallel", "parallel")), name="attn_sample",
    )(q, cache_k, cache_v, kbuf, vbuf, lamp, sub)


def _gla_tables(c):
    levels = int(math.log2(c))
    t = np.arange(c)[:, None]
    r = np.arange(c)[None, :]
    blocks = [(r <= t)]
    for p in range(levels):
        s = 1 << p
        m = (t >> (p + 1) << (p + 1)) + s - 1
        upper = ((t >> p) & 1) == 1
        blocks.append(np.where(upper, (r > m) & (r <= t), (r > t) & (r <= m)))
    l_rows = np.concatenate(blocks, axis=0).astype(np.float32)
    tail_t = (np.arange(c)[:, None] > np.arange(c)[None, :]).astype(np.float32)
    rt = np.concatenate([np.ones((c, G_DV), np.float32), tail_t] + [b.T.astype(np.float32) for b in blocks[1:]],
                        axis=1)
    rt = np.pad(rt, ((0, 0), (0, -rt.shape[1] % LANES)))
    i, j = t, r
    masks = [i == j] + [(((i ^ j) >> p) == 1) & (((i >> p) & 1) == 1) for p in range(levels)]
    return (jnp.asarray(np.concatenate([l_rows, l_rows], axis=1), BF16),
            jnp.asarray(np.concatenate([rt, rt], axis=0), BF16),
            jnp.asarray(np.stack(masks).astype(np.float32)), levels)


def _gla_kernel(*refs, c, nch, levels, has_s0, aliased):
    qk_ref, v_ref, r_ref, g2_ref, kT_ref, gTh_ref, gTl_ref, l2_ref, rt_ref, pm_ref, gn_ref = refs[:11]
    pos = 11
    s0_ref = None
    if has_s0:
        s0_ref = refs[pos]
        pos += 1
    if aliased:
        pos += 1
    o_ref, sout_ref, s_scr = refs[pos:pos + 3]
    t = pl.program_id(1)

    @pl.when(t == 0)
    def _():
        if has_s0:
            s_scr[...] = s0_ref[...]
        else:
            s_scr[...] = jnp.zeros(s_scr.shape, F32)

    gn = gn_ref[...]

    def chunk(r0):
        rows = pl.ds(r0, c)
        g2 = g2_ref[rows, :]
        x_all = _dot(l2_ref[...], jnp.concatenate([g2[:, :G_KW], g2[:, G_KW:]], axis=0))
        gT = jnp.concatenate([gTh_ref[:, rows], gTl_ref[:, rows]], axis=1)
        xT = _dot(gT, rt_ref[...])
        for h in range(G_HEADS):
            dk = slice(h * G_DK, (h + 1) * G_DK)
            dv = slice(h * G_DV, (h + 1) * G_DV)
            qb = qk_ref[rows, dk]
            kTb = kT_ref[dk, rows]
            q = qb.astype(F32)
            kT = kTb.astype(F32)
            v = v_ref[rows, dv]
            a = pm_ref[0] * _dot(qb, kTb)
            for p in range(levels):
                e = jnp.exp2(x_all[(1 + p) * c:(2 + p) * c, dk])
                eT = jnp.exp2(xT[dk, G_DV + (1 + p) * c:G_DV + (2 + p) * c])
                a = a + pm_ref[1 + p] * _dot((q * e).astype(BF16), (kT * eT).astype(BF16))
            s_old = s_scr[h]
            qe = (q * jnp.exp2(x_all[0:c, dk])).astype(BF16)
            o = _dot(qe, s_old.astype(BF16)) + _dot(a.astype(BF16), v)
            rg = r_ref[rows, dv].astype(F32)
            o_ref[rows, dv] = (_rms(o, gn) * (rg * _sigmoid(rg))).astype(BF16)
            kdT = (kT * jnp.exp2(xT[dk, G_DV:G_DV + c])).astype(BF16)
            s_scr[h] = jnp.exp2(xT[dk, 0:G_DV]) * s_old + _dot(kdT, v)

    for i in range(nch):
        chunk(i * c)

    @pl.when(t == pl.num_programs(1) - 1)
    def _():
        sout_ref[...] = s_scr[...]


def _gla(qkg, vg, rg, g2, kT, gTh, gTl, gn, s0, sbuf, layer, depth, batch, seq, c, cg):
    n = batch * seq
    nb = seq // cg
    l2, rt, pmask, levels = _gla_tables(c)
    has_s0 = s0 is not None
    aliased = sbuf is not None
    row = pl.BlockSpec((cg, 1024), lambda b, t: (b * nb + t, 0))
    if kT.ndim == 3:
        col = pl.BlockSpec((None, G_KW, cg), lambda b, t: (b, 0, t))
    else:
        col = pl.BlockSpec((G_KW, cg), lambda b, t: (0, b * nb + t))
    in_specs = [row, row, row, row, col, col, col, _const_spec(l2.shape), _const_spec(rt.shape),
                _const_spec(pmask.shape), _const_spec((1, G_DV))]
    args = [qkg, vg, rg, g2, kT, gTh, gTl, l2, rt, pmask, gn]
    if has_s0:
        in_specs.append(pl.BlockSpec((None, None, G_HEADS, G_DK, G_DV), lambda b, t: (layer, b, 0, 0, 0)))
        args.append(s0)
    io_alias = {}
    if aliased:
        in_specs.append(pl.BlockSpec(memory_space=pl.ANY))
        args.append(sbuf)
        io_alias = {len(args) - 1: 1}
    s_spec = pl.BlockSpec((None, None, G_HEADS, G_DK, G_DV), lambda b, t: (layer, b, 0, 0, 0))
    return pl.pallas_call(
        functools.partial(_gla_kernel, c=c, nch=cg // c, levels=levels, has_s0=has_s0, aliased=aliased),
        grid=(batch, nb), in_specs=in_specs, out_specs=[row, s_spec],
        out_shape=[jax.ShapeDtypeStruct((n, 1024), BF16),
                   jax.ShapeDtypeStruct((depth, batch, G_HEADS, G_DK, G_DV), F32)],
        scratch_shapes=[pltpu.VMEM((G_HEADS, G_DK, G_DV), F32)],
        input_output_aliases=io_alias,
        compiler_params=_params(("parallel", "arbitrary")), name="gla",
    )(*args)


def _merge_kernel(x_ref, oa_ref, og_ref, sga_ref, sgb_ref, wpa_ref, wpb_ref, wo_ref, o_ref):
    ua = _dot(oa_ref[...], wpa_ref[...])
    ub = _dot(og_ref[...], wpb_ref[...])
    mixed = sga_ref[...].astype(F32) * ua + sgb_ref[...].astype(F32) * ub
    o_ref[...] = x_ref[...] + _dot(mixed.astype(BF16), wo_ref[...])


def _merge(x, oa, og, sga, sgb, lw, tm):
    n = x.shape[0]
    row = pl.BlockSpec((tm, D_MODEL), lambda i: (i, 0))
    wspec = _const_spec((D_MODEL, D_MODEL))
    return pl.pallas_call(
        _merge_kernel, grid=(n // tm,),
        in_specs=[row, row, row, row, row, wspec, wspec, wspec], out_specs=row,
        out_shape=jax.ShapeDtypeStruct(x.shape, F32),
        compiler_params=_params(("parallel",)), name="merge",
    )(x, oa, og, sga, sgb, lw["w_pa"], lw["w_pb"], lw["w_o"])


def _ffn_ple_kernel(*refs, ff_chunks, final):
    (x_ref, pe_ref, nf_ref, w1_ref, w3_ref, w2_ref, np_ref, wpg_ref, wpe_ref) = refs[:9]
    fn_ref = refs[9] if final else None
    o_ref = refs[-1]
    x = x_ref[...]
    h = _rms(x, nf_ref[...]).astype(BF16)
    acc = x
    for c0, c1 in ff_chunks:
        a = _dot(h, w1_ref[:, c0:c1])
        u = (a * _sigmoid(a)) * _dot(h, w3_ref[:, c0:c1])
        acc = acc + _dot(u.astype(BF16), w2_ref[c0:c1, :])
    h2 = _rms(acc, np_ref[...]).astype(BF16)
    gate = _sigmoid(_dot(h2, wpg_ref[...]))
    y = acc + gate * _dot(pe_ref[...].astype(BF16), wpe_ref[...])
    if final:
        y = _rms(y, fn_ref[...])
    o_ref[...] = y


def _ffn_ple(x, pe, lw, final_norm, layer, tm):
    n = x.shape[0]
    d_ff = lw["w_f1"].shape[1]
    half = d_ff // 2
    ff_chunks = ((0, half), (half, d_ff)) if half % LANES == 0 else ((0, d_ff),)
    final = final_norm is not None
    row = pl.BlockSpec((tm, D_MODEL), lambda i: (i, 0))
    in_specs = [row, pl.BlockSpec((None, tm, PLE_DIM), lambda i: (layer, i, 0)), _const_spec((1, D_MODEL)),
                _const_spec((D_MODEL, d_ff)), _const_spec((D_MODEL, d_ff)), _const_spec((d_ff, D_MODEL)),
                _const_spec((1, D_MODEL)), _const_spec((D_MODEL, D_MODEL)), _const_spec((PLE_DIM, D_MODEL))]
    args = [x, pe, lw["norm_ffn"], lw["w_f1"], lw["w_f3"], lw["w_f2"], lw["norm_ple"], lw["w_ple_gate"],
            lw["w_ple"]]
    if final:
        in_specs.append(_const_spec((1, D_MODEL)))
        args.append(final_norm)
    return pl.pallas_call(
        functools.partial(_ffn_ple_kernel, ff_chunks=ff_chunks, final=final), grid=(n // tm,),
        in_specs=in_specs, out_specs=row, out_shape=jax.ShapeDtypeStruct(x.shape, F32),
        compiler_params=_params(("parallel",)), name="ffn_ple",
    )(*args)


def _rope_tables(pos):
    half = ROPE_DIM // 2
    inv = ROPE_THETA ** (-jnp.arange(0, ROPE_DIM, 2, dtype=F32) / ROPE_DIM)
    ang = pos.astype(F32)[:, None] * inv[None, :]
    cos, sin = jnp.cos(ang), jnp.sin(ang)
    n = pos.shape[0]
    pad = jnp.zeros((n, A_DH - ROPE_DIM), F32)
    cos64 = jnp.concatenate([cos, cos, pad + 1.0], axis=1)
    s1_64 = jnp.concatenate([jnp.zeros((n, half), F32), sin, pad], axis=1)
    s2_64 = jnp.concatenate([-sin, jnp.zeros((n, half), F32), pad], axis=1)
    return tuple(jnp.concatenate([t, t], axis=1) for t in (cos64, s1_64, s2_64))


def _prep_weights(w_in, norm_mix, w_ga2, b_ga, w_pa, w_pb, w_o, norm_ffn, w_f1, w_f3, w_f2, norm_ple, w_ple,
                  w_ple_gate):
    depth = w_in.shape[0]
    w_main = jnp.concatenate([w_in[:, :, :COL_AG], w_in[:, :, COL_AG + G_RANK:]], axis=2).astype(BF16)
    w_ag = jnp.pad(w_in[:, :, COL_AG:COL_AG + G_RANK], ((0, 0), (0, 0), (0, LANES - G_RANK))).astype(BF16)
    w_ga = jnp.pad(w_ga2, ((0, 0), (0, LANES - G_RANK), (0, 0))).astype(BF16)
    layers = []
    for l in range(depth):
        layers.append(dict(
            depth=depth,
            norm_mix=norm_mix[l][None], w_main=w_main[l], w_ag=w_ag[l], w_ga=w_ga[l], b_ga=b_ga[l][None],
            w_pa=w_pa[l].astype(BF16), w_pb=w_pb[l].astype(BF16), w_o=w_o[l].astype(BF16),
            norm_ffn=norm_ffn[l][None], w_f1=w_f1[l].astype(BF16), w_f3=w_f3[l].astype(BF16),
            w_f2=w_f2[l].astype(BF16), norm_ple=norm_ple[l][None], w_ple=w_ple[l].astype(BF16),
            w_ple_gate=w_ple_gate[l].astype(BF16)))
    return layers


def kernel(x_prompt, x_sample, p_prompt, p_sample, cache_k, cache_v, state_gla, w_in, norm_mix, lambda_q1,
           lambda_k1, lambda_q2, lambda_k2, a_subln, w_pa, w_ga2, b_ga, g_norm, w_pb, w_o, norm_ffn, w_f1, w_f3,
           w_f2, norm_ple, w_ple, w_ple_gate, final_norm):
    depth = w_in.shape[0]
    batch, seq, _ = x_prompt.shape
    dbatch, ts, _ = x_sample.shape
    past = cache_k.shape[2]
    n_p, n_s = batch * seq, dbatch * ts
    assert ts == CHUNK and seq % ATTN_TILE == 0 and seq % GLA_BLOCK == 0 and n_p % TOKEN_TILE == 0 and n_s % 8 == 0

    layers = _prep_weights(w_in, norm_mix, w_ga2, b_ga, w_pa, w_pb, w_o, norm_ffn, w_f1, w_f3, w_f2, norm_ple,
                           w_ple, w_ple_gate)
    rope_p = _rope_tables(jnp.arange(seq))
    rope_s = _rope_tables(jnp.tile(past + jnp.arange(ts), dbatch))
    lamp = jnp.pad(jnp.stack([lambda_q1, lambda_k1, lambda_q2, lambda_k2], axis=1),
                   ((0, 0), (0, 4), (0, LANES - A_DH)))
    fnorm = final_norm[None]
    tm_p = TOKEN_TILE
    tm_s = min(TOKEN_TILE, n_s)
    pe_p = p_prompt.reshape(depth, n_p, PLE_DIM)
    pe_s = p_sample.reshape(depth, n_s, PLE_DIM)
    cache_k4 = cache_k.reshape(depth, dbatch, past, A_HEADS * A_DV)
    cache_v4 = cache_v.reshape(depth, dbatch, past, A_HEADS * A_DV)

    xp = x_prompt.reshape(n_p, D_MODEL)
    xs = x_sample.reshape(n_s, D_MODEL)
    kp = vp = sp = ks = vs = ss = None
    for l in range(depth):
        lw = layers[l]
        last = fnorm if l == depth - 1 else None
        sub = a_subln[l][None]
        gn = g_norm[l][None]

        q, kp, vp, qkg, vg, rg, g2, kT, gTh, gTl, sga, sgb = _in_proj(xp, lw, rope_p, l, kp, vp, tm_p)
        oa = _attn_seq(q, kp, vp, lamp[l], a_subln[l][:, None], l, batch, seq, ATTN_TILE, ATTN_HEADS_PER_STEP)
        og, sp = _gla(qkg, vg, rg, g2, kT, gTh, gTl, gn, None, sp, l, depth, batch, seq, GLA_CHUNK, GLA_BLOCK)
        xp = _merge(xp, oa, og, sga, sgb, lw, tm_p)
        xp = _ffn_ple(xp, pe_p, lw, last, l, tm_p)

        q, ks, vs, qkg, vg, rg, g2, kT, gTh, gTl, sga, sgb = _in_proj(xs, lw, rope_s, l, ks, vs, tm_s)
        oa = _attn_sample(q, cache_k4, cache_v4, ks, vs, lamp[l], sub, l, dbatch, ts)
        by_batch = lambda a: a.reshape(G_KW, dbatch, ts).transpose(1, 0, 2)
        og, ss = _gla(qkg, vg, rg, g2, by_batch(kT), by_batch(gTh), by_batch(gTl), gn, state_gla, ss, l, depth,
                      dbatch, ts, ts, ts)
        xs = _merge(xs, oa, og, sga, sgb, lw, tm_s)
        xs = _ffn_ple(xs, pe_s, lw, last, l, tm_s)

    shape_kv = lambda a, b, t: a.reshape(depth, b, t, A_HEADS, A_DV)
    return (xp.reshape(batch, seq, D_MODEL), xs.reshape(dbatch, ts, D_MODEL),
            shape_kv(kp, batch, seq), shape_kv(vp, batch, seq), sp,
            shape_kv(ks, dbatch, ts), shape_kv(vs, dbatch, ts), ss)
```

```python
import functools
import math

import numpy as np
import jax
import jax.numpy as jnp
from jax import lax
from jax.experimental import pallas as pl
from jax.experimental.pallas import tpu as pltpu

F32 = jnp.float32
BF16 = jnp.bfloat16

D_MODEL = 1024
CHUNK = 64
EPS = 1e-6
A_HEADS = 8
A_DH = 64
A_DV = 128
ROPE_DIM = 16
ROPE_THETA = 500000.0
G_HEADS = 4
G_DK = 128
G_DV = 256
G_KW = 512
G_RANK = 16
G_NORMALIZER = 16.0
PLE_DIM = 256
LANES = 128
ONES_ROWS = 16
VMEM_LIMIT = 56 * 1024 * 1024

COL_QA, COL_KA, COL_VA, COL_QKG, COL_VG, COL_RG, COL_GA, COL_GB, W_MAIN = (1024 * i for i in range(9))
COL_AG = COL_GA

TOKEN_TILE = 512
ATTN_TILE = 512
ATTN_HEADS_PER_STEP = 2
GLA_CHUNK = 128
GLA_BLOCK = 512

Q_SCALE = A_DH ** -0.5 * math.log2(math.e)
G_SCALE = math.log2(math.e) / G_NORMALIZER
NT_DIMS = (((1,), (1,)), ((), ()))


def _dot(a, b):
    return jnp.dot(a, b, preferred_element_type=F32)


def _dot_nt(a, b):
    return lax.dot_general(a, b, NT_DIMS, preferred_element_type=F32)


def _sigmoid(x):
    return 1.0 / (1.0 + jnp.exp(-x))


def _log_sigmoid(x):
    return jnp.minimum(x, 0.0) - jnp.log(1.0 + jnp.exp(-jnp.abs(x)))


def _rms(x, w):
    return x * lax.rsqrt(jnp.mean(x * x, axis=-1, keepdims=True) + EPS) * w


def _const_spec(shape):
    nd = len(shape)
    return pl.BlockSpec(shape, lambda *_: (0,) * nd, pipeline_mode=pl.Buffered(1))


def _params(sem):
    return pltpu.CompilerParams(dimension_semantics=sem, vmem_limit_bytes=VMEM_LIMIT)


def _in_proj_kernel(*refs, aliased):
    x_ref, nw_ref, wm_ref, wag_ref, wga_ref, bga_ref, cos_ref, s1_ref, s2_ref = refs[:9]
    outs = refs[9 + (2 if aliased else 0):]
    (q_ref, k_ref, v_ref, qkg_ref, vg_ref, rg_ref, g2_ref, kT_ref, gTh_ref, gTl_ref,
     sga_ref, sgb_ref) = outs

    h = _rms(x_ref[...], nw_ref[...]).astype(BF16)

    def mm(c0, c1):
        return _dot(h, wm_ref[:, c0:c1])

    cos, s1, s2 = cos_ref[...], s1_ref[...], s2_ref[...]

    def rope_store(z, out_ref, scale):
        for c in range(z.shape[1] // LANES):
            zc = z[:, c * LANES:(c + 1) * LANES]
            r = zc * cos + pltpu.roll(zc, 8, 1) * s1 + pltpu.roll(zc, LANES - 8, 1) * s2
            if scale != 1.0:
                r = r * scale
            out_ref[:, c * LANES:(c + 1) * LANES] = r.astype(out_ref.dtype)

    rope_store(mm(COL_QA, COL_KA), q_ref, Q_SCALE)
    rope_store(mm(COL_KA, COL_VA), k_ref, 1.0)
    v_ref[...] = mm(COL_VA, COL_QKG)
    z = mm(COL_QKG, COL_VG)
    qkg_ref[:, :G_KW] = (z[:, :G_KW] * (G_DK ** -0.5)).astype(BF16)
    qkg_ref[:, G_KW:] = z[:, G_KW:].astype(BF16)
    kT_ref[...] = z[:, G_KW:].T.astype(BF16)
    vg_ref[...] = mm(COL_VG, COL_RG).astype(BF16)
    rg_ref[...] = mm(COL_RG, COL_GA).astype(BF16)
    sga_ref[...] = _sigmoid(mm(COL_GA, COL_GB)).astype(BF16)
    sgb_ref[...] = _sigmoid(mm(COL_GB, W_MAIN)).astype(BF16)

    ag = _dot(h, wag_ref[...]).astype(BF16)
    g = _log_sigmoid(_dot(ag, wga_ref[...]) + bga_ref[...]) * G_SCALE
    g_hi = g.astype(BF16)
    g2_ref[:, :G_KW] = g_hi
    g2_ref[:, G_KW:] = (g - g_hi.astype(F32)).astype(BF16)
    gT = g.T
    gT_hi = gT.astype(BF16)
    gTh_ref[...] = gT_hi
    gTl_ref[...] = (gT - gT_hi.astype(F32)).astype(BF16)


def _in_proj(x, lw, rope, layer, kbuf, vbuf, tm):
    n = x.shape[0]
    nt = n // tm
    cos, s1, s2 = rope
    nrb = cos.shape[0] // tm
    aliased = kbuf is not None
    depth = lw["depth"]

    row = lambda w: pl.BlockSpec((tm, w), lambda i: (i, 0))
    col = lambda h: pl.BlockSpec((h, tm), lambda i: (0, i))
    tab = pl.BlockSpec((tm, LANES), lambda i: (i % nrb, 0))
    in_specs = [row(D_MODEL), _const_spec((1, D_MODEL)), _const_spec((D_MODEL, W_MAIN)),
                _const_spec((D_MODEL, LANES)), _const_spec((LANES, G_KW)), _const_spec((1, G_KW)), tab, tab, tab]
    args = [x, lw["norm_mix"], lw["w_main"], lw["w_ag"], lw["w_ga"], lw["b_ga"], cos, s1, s2]
    io_alias = {}
    if aliased:
        in_specs += [pl.BlockSpec(memory_space=pl.ANY)] * 2
        args += [kbuf, vbuf]
        io_alias = {9: 1, 10: 2}
    kv_spec = pl.BlockSpec((None, tm, D_MODEL), lambda i: (layer, i, 0))
    out_specs = [row(D_MODEL), kv_spec, kv_spec, row(1024), row(1024), row(1024), row(1024),
                 col(G_KW), col(G_KW), col(G_KW), row(D_MODEL), row(D_MODEL)]
    sd = jax.ShapeDtypeStruct
    out_shape = [sd((n, D_MODEL), BF16), sd((depth, n, D_MODEL), F32), sd((depth, n, D_MODEL), F32),
                 sd((n, 1024), BF16), sd((n, 1024), BF16), sd((n, 1024), BF16), sd((n, 1024), BF16),
                 sd((G_KW, n), BF16), sd((G_KW, n), BF16), sd((G_KW, n), BF16),
                 sd((n, D_MODEL), BF16), sd((n, D_MODEL), BF16)]
    return pl.pallas_call(
        functools.partial(_in_proj_kernel, aliased=aliased),
        grid=(nt,), in_specs=in_specs, out_specs=out_specs, out_shape=out_shape,
        input_output_aliases=io_alias, compiler_params=_params(("parallel",)), name="in_proj",
    )(*args)


def _lambda(lamp_ref, lam_init):
    lp = lamp_ref[...]
    d1 = jnp.sum(lp[0:1] * lp[1:2], axis=-1, keepdims=True)
    d2 = jnp.sum(lp[2:3] * lp[3:4], axis=-1, keepdims=True)
    return jnp.exp(d1) - jnp.exp(d2) + lam_init


def _attn_finish(o1, o2, lamp_ref, sub_ref, lam_init):
    o = o1 - _lambda(lamp_ref, lam_init) * o2
    return (_rms(o, sub_ref[...]) * (1.0 - lam_init)).astype(BF16)


def _attn_seq_kernel(q_ref, k_ref, v_ref, lamp_ref, subT_ref, bias_ref, o_ref, kb, vT, qT, m_scr, acc, s_scr,
                     mt_scr,
                     *, tq, nh, lam_init):
    seq = k_ref.shape[0]
    nq = seq // tq
    for h in range(nh):
        k = k_ref[:, h * A_DV:(h + 1) * A_DV]
        kb[h, 0] = k[:, :A_DH].astype(BF16)
        kb[h, 1] = k[:, A_DH:].astype(BF16)
        vT[h, 0:A_DV, :] = v_ref[:, h * A_DV:(h + 1) * A_DV].T.astype(BF16)
        vT[h, A_DV:, :] = jnp.ones((ONES_ROWS, seq), BF16)
        qt = q_ref[:, h * A_DV:(h + 1) * A_DV].astype(F32).T
        qT[h, 0] = qt[:A_DH].astype(BF16)
        qT[h, 1] = qt[A_DH:].astype(BF16)

    lam = _lambda(lamp_ref, lam_init)
    tiles = [(qi, t) for qi in range(nq) for t in range(qi + 1)]
    hq = tq // 2

    def scores(n):
        qi, t = tiles[n]
        slot = n % 2
        for h in range(nh):
            for mp in range(2):
                c = 2 * h + mp
                q_t = qT[h, mp, :, qi * tq:(qi + 1) * tq]
                if t == qi:
                    s_a = _dot(kb[h, mp, t * tq:t * tq + hq, :], q_t) + bias_ref[:, :tq]
                    s_b = _dot(kb[h, mp, t * tq + hq:(t + 1) * tq, :], q_t[:, hq:]) + bias_ref[:, tq:]
                    s_scr[slot, c, :hq, :] = s_a
                    s_scr[slot, c, hq:, hq:] = s_b
                    m_a = jnp.max(s_a, axis=0, keepdims=True)
                    mt_scr[slot, c] = jnp.concatenate(
                        [m_a[:, :hq], jnp.maximum(m_a[:, hq:], jnp.max(s_b, axis=0, keepdims=True))], axis=1)
                else:
                    s = _dot(kb[h, mp, t * tq:(t + 1) * tq, :], q_t)
                    s_scr[slot, c] = s
                    mt_scr[slot, c] = jnp.max(s, axis=0, keepdims=True)

    def softmax_pv(n):
        qi, t = tiles[n]
        slot = n % 2
        for h in range(nh):
            vt = vT[h, :, t * tq:(t + 1) * tq]
            for mp in range(2):
                c = 2 * h + mp
                m_tile = mt_scr[slot, c]
                m_new = m_tile if t == 0 else jnp.maximum(m_scr[c], m_tile)
                if t == qi:
                    p_a = jnp.exp2(s_scr[slot, c, :hq, :] - m_new).astype(BF16)
                    p_b = jnp.exp2(s_scr[slot, c, hq:, hq:] - m_new[:, hq:]).astype(BF16)
                    pv_a = _dot(vt[:, :hq], p_a)
                    pv = jnp.concatenate([pv_a[:, :hq], pv_a[:, hq:] + _dot(vt[:, hq:], p_b)], axis=1)
                else:
                    pv = _dot(vt, jnp.exp2(s_scr[slot, c] - m_new).astype(BF16))
                acc[c] = pv if t == 0 else jnp.exp2(m_scr[c] - m_new) * acc[c] + pv
                m_scr[c] = m_new

    def finish(qi):
        for h in range(nh):
            a1, a2 = acc[2 * h], acc[2 * h + 1]
            oT = a1[:A_DV] / a1[A_DV:A_DV + 1] - lam * (a2[:A_DV] / a2[A_DV:A_DV + 1])
            ms = jnp.mean(oT * oT, axis=0, keepdims=True)
            yT = oT * lax.rsqrt(ms + EPS) * subT_ref[...] * (1.0 - lam_init)
            o_ref[qi * tq:(qi + 1) * tq, h * A_DV:(h + 1) * A_DV] = yT.T.astype(BF16)

    scores(0)
    for n, (qi, t) in enumerate(tiles):
        if n + 1 < len(tiles):
            scores(n + 1)
        softmax_pv(n)
        if t == qi:
            finish(qi)


def _attn_seq(q, kbuf, vbuf, lamp, subT, layer, batch, seq, tq, nh):
    lam_init = 0.8 - 0.6 * math.exp(-0.3 * layer)
    kv_spec = pl.BlockSpec((None, seq, nh * A_DV), lambda b, h: (layer, b, h))
    qo_spec = pl.BlockSpec((seq, nh * A_DV), lambda b, h: (b, h))
    vm = pltpu.VMEM
    hq = tq // 2
    key_chunk = np.arange(hq)[:, None] // CHUNK
    qry_chunk = np.arange(tq)[None, :] // CHUNK
    visible = np.concatenate([key_chunk <= qry_chunk, key_chunk <= qry_chunk[:, :hq]], axis=1)
    bias = jnp.asarray(np.where(visible, 0.0, -np.inf), F32)
    return pl.pallas_call(
        functools.partial(_attn_seq_kernel, tq=tq, nh=nh, lam_init=lam_init),
        grid=(batch, A_HEADS // nh),
        in_specs=[qo_spec, kv_spec, kv_spec,
                  pl.BlockSpec((8, LANES), lambda b, h: (0, 0)), pl.BlockSpec((A_DV, 1), lambda b, h: (0, 0)),
                  _const_spec(bias.shape)],
        out_specs=qo_spec,
        out_shape=jax.ShapeDtypeStruct(q.shape, BF16),
        scratch_shapes=[vm((nh, 2, seq, A_DH), BF16), vm((nh, A_DV + ONES_ROWS, seq), BF16),
                        vm((nh, 2, A_DH, seq), BF16), vm((2 * nh, 1, tq), F32),
                        vm((2 * nh, A_DV + ONES_ROWS, tq), F32), vm((2, 2 * nh, tq, tq), F32),
                        vm((2, 2 * nh, 1, tq), F32)],
        compiler_params=_params(("parallel", "parallel")), name="attn_prompt",
    )(q, kbuf, vbuf, lamp, subT, bias)


def _attn_sample_kernel(q_ref, kc_ref, vc_ref, kn_ref, vn_ref, lamp_ref, sub_ref, o_ref, *, lam_init):
    q = q_ref[...]
    kc = kc_ref[...].astype(BF16)
    vc = vc_ref[...].astype(BF16)
    kn = kn_ref[...].astype(BF16)
    vn = vn_ref[...].astype(BF16)
    outs = []
    for lo in (0, A_DH):
        qq = q[:, lo:lo + A_DH]
        sc = _dot_nt(qq, kc[:, lo:lo + A_DH])
        sn = _dot_nt(qq, kn[:, lo:lo + A_DH])
        m = jnp.maximum(jnp.max(sc, axis=-1, keepdims=True), jnp.max(sn, axis=-1, keepdims=True))
        pc = jnp.exp2(sc - m)
        pn = jnp.exp2(sn - m)
        denom = jnp.sum(pc, axis=-1, keepdims=True) + jnp.sum(pn, axis=-1, keepdims=True)
        outs.append((_dot(pc.astype(BF16), vc) + _dot(pn.astype(BF16), vn)) / denom)
    o_ref[...] = _attn_finish(outs[0], outs[1], lamp_ref, sub_ref, lam_init)


def _attn_sample(q, cache_k, cache_v, kbuf, vbuf, lamp, sub, layer, batch, ts):
    past = cache_k.shape[2]
    lam_init = 0.8 - 0.6 * math.exp(-0.3 * layer)
    cache_spec = pl.BlockSpec((None, None, past, A_DV), lambda b, h: (layer, b, 0, h))
    new_spec = pl.BlockSpec((None, ts, A_DV), lambda b, h: (layer, b, h))
    qo_spec = pl.BlockSpec((ts, A_DV), lambda b, h: (b, h))
    return pl.pallas_call(
        functools.partial(_attn_sample_kernel, lam_init=lam_init),
        grid=(batch, A_HEADS),
        in_specs=[qo_spec, cache_spec, cache_spec, new_spec, new_spec,
                  pl.BlockSpec((8, LANES), lambda b, h: (0, 0)), pl.BlockSpec((1, A_DV), lambda b, h: (0, 0))],
        out_specs=qo_spec,
        out_shape=jax.ShapeDtypeStruct(q.shape, BF16),
        compiler_params=_params(("parallel", "parallel")), name="attn_sample",
    )(q, cache_k, cache_v, kbuf, vbuf, lamp, sub)


def _gla_tables(c):
    levels = int(math.log2(c))
    t = np.arange(c)[:, None]
    r = np.arange(c)[None, :]
    blocks = [(r <= t)]
    for p in range(levels):
        s = 1 << p
        m = (t >> (p + 1) << (p + 1)) + s - 1
        upper = ((t >> p) & 1) == 1
        blocks.append(np.where(upper, (r > m) & (r <= t), (r > t) & (r <= m)))
    l_rows = np.concatenate(blocks, axis=0).astype(np.float32)
    tail_t = (np.arange(c)[:, None] > np.arange(c)[None, :]).astype(np.float32)
    rt = np.concatenate([np.ones((c, G_DV), np.float32), tail_t] + [b.T.astype(np.float32) for b in blocks[1:]],
                        axis=1)
    rt = np.pad(rt, ((0, 0), (0, -rt.shape[1] % LANES)))
    i, j = t, r
    masks = [i == j] + [(((i ^ j) >> p) == 1) & (((i >> p) & 1) == 1) for p in range(levels)]
    return (jnp.asarray(np.concatenate([l_rows, l_rows], axis=1), BF16),
            jnp.asarray(np.concatenate([rt, rt], axis=0), BF16),
            jnp.asarray(np.stack(masks).astype(np.float32)), levels)


def _gla_kernel(*refs, c, nch, levels, has_s0, aliased):
    qk_ref, v_ref, r_ref, g2_ref, kT_ref, gTh_ref, gTl_ref, l2_ref, rt_ref, pm_ref, gn_ref = refs[:11]
    pos = 11
    s0_ref = None
    if has_s0:
        s0_ref = refs[pos]
        pos += 1
    if aliased:
        pos += 1
    o_ref, sout_ref, s_scr = refs[pos:pos + 3]
    t = pl.program_id(1)

    @pl.when(t == 0)
    def _():
        if has_s0:
            s_scr[...] = s0_ref[...]
        else:
            s_scr[...] = jnp.zeros(s_scr.shape, F32)

    gn = gn_ref[...]

    def chunk(r0):
        rows = pl.ds(r0, c)
        g2 = g2_ref[rows, :]
        x_all = _dot(l2_ref[...], jnp.concatenate([g2[:, :G_KW], g2[:, G_KW:]], axis=0))
        gT = jnp.concatenate([gTh_ref[:, rows], gTl_ref[:, rows]], axis=1)
        xT = _dot(gT, rt_ref[...])
        for h in range(G_HEADS):
            dk = slice(h * G_DK, (h + 1) * G_DK)
            dv = slice(h * G_DV, (h + 1) * G_DV)
            qb = qk_ref[rows, dk]
            kTb = kT_ref[dk, rows]
            q = qb.astype(F32)
            kT = kTb.astype(F32)
            v = v_ref[rows, dv]
            a = pm_ref[0] * _dot(qb, kTb)
            for p in range(levels):
                e = jnp.exp2(x_all[(1 + p) * c:(2 + p) * c, dk])
                eT = jnp.exp2(xT[dk, G_DV + (1 + p) * c:G_DV + (2 + p) * c])
                a = a + pm_ref[1 + p] * _dot((q * e).astype(BF16), (kT * eT).astype(BF16))
            s_old = s_scr[h]
            qe = (q * jnp.exp2(x_all[0:c, dk])).astype(BF16)
            o = _dot(qe, s_old.astype(BF16)) + _dot(a.astype(BF16), v)
            rg = r_ref[rows, dv].astype(F32)
            o_ref[rows, dv] = (_rms(o, gn) * (rg * _sigmoid(rg))).astype(BF16)
            kdT = (kT * jnp.exp2(xT[dk, G_DV:G_DV + c])).astype(BF16)
            s_scr[h] = jnp.exp2(xT[dk, 0:G_DV]) * s_old + _dot(kdT, v)

    for i in range(nch):
        chunk(i * c)

    @pl.when(t == pl.num_programs(1) - 1)
    def _():
        sout_ref[...] = s_scr[...]


def _gla(qkg, vg, rg, g2, kT, gTh, gTl, gn, s0, sbuf, layer, depth, batch, seq, c, cg):
    n = batch * seq
    nb = seq // cg
    l2, rt, pmask, levels = _gla_tables(c)
    has_s0 = s0 is not None
    aliased = sbuf is not None
    row = pl.BlockSpec((cg, 1024), lambda b, t: (b * nb + t, 0))
    if kT.ndim == 3:
        col = pl.BlockSpec((None, G_KW, cg), lambda b, t: (b, 0, t))
    else:
        col = pl.BlockSpec((G_KW, cg), lambda b, t: (0, b * nb + t))
    in_specs = [row, row, row, row, col, col, col, _const_spec(l2.shape), _const_spec(rt.shape),
                _const_spec(pmask.shape), _const_spec((1, G_DV))]
    args = [qkg, vg, rg, g2, kT, gTh, gTl, l2, rt, pmask, gn]
    if has_s0:
        in_specs.append(pl.BlockSpec((None, None, G_HEADS, G_DK, G_DV), lambda b, t: (layer, b, 0, 0, 0)))
        args.append(s0)
    io_alias = {}
    if aliased:
        in_specs.append(pl.BlockSpec(memory_space=pl.ANY))
        args.append(sbuf)
        io_alias = {len(args) - 1: 1}
    s_spec = pl.BlockSpec((None, None, G_HEADS, G_DK, G_DV), lambda b, t: (layer, b, 0, 0, 0))
    return pl.pallas_call(
        functools.partial(_gla_kernel, c=c, nch=cg // c, levels=levels, has_s0=has_s0, aliased=aliased),
        grid=(batch, nb), in_specs=in_specs, out_specs=[row, s_spec],
        out_shape=[jax.ShapeDtypeStruct((n, 1024), BF16),
                   jax.ShapeDtypeStruct((depth, batch, G_HEADS, G_DK, G_DV), F32)],
        scratch_shapes=[pltpu.VMEM((G_HEADS, G_DK, G_DV), F32)],
        input_output_aliases=io_alias,
        compiler_params=_params(("parallel", "arbitrary")), name="gla",
    )(*args)


def _merge_kernel(x_ref, oa_ref, og_ref, sga_ref, sgb_ref, wpa_ref, wpb_ref, wo_ref, o_ref):
    ua = _dot(oa_ref[...], wpa_ref[...])
    ub = _dot(og_ref[...], wpb_ref[...])
    mixed = sga_ref[...].astype(F32) * ua + sgb_ref[...].astype(F32) * ub
    o_ref[...] = x_ref[...] + _dot(mixed.astype(BF16), wo_ref[...])


def _merge(x, oa, og, sga, sgb, lw, tm):
    n = x.shape[0]
    row = pl.BlockSpec((tm, D_MODEL), lambda i: (i, 0))
    wspec = _const_spec((D_MODEL, D_MODEL))
    return pl.pallas_call(
        _merge_kernel, grid=(n // tm,),
        in_specs=[row, row, row, row, row, wspec, wspec, wspec], out_specs=row,
        out_shape=jax.ShapeDtypeStruct(x.shape, F32),
        compiler_params=_params(("parallel",)), name="merge",
    )(x, oa, og, sga, sgb, lw["w_pa"], lw["w_pb"], lw["w_o"])


def _ffn_ple_kernel(*refs, ff_chunks, final):
    (x_ref, pe_ref, nf_ref, w1_ref, w3_ref, w2_ref, np_ref, wpg_ref, wpe_ref) = refs[:9]
    fn_ref = refs[9] if final else None
    o_ref = refs[-1]
    x = x_ref[...]
    h = _rms(x, nf_ref[...]).astype(BF16)
    acc = x
    for c0, c1 in ff_chunks:
        a = _dot(h, w1_ref[:, c0:c1])
        u = (a * _sigmoid(a)) * _dot(h, w3_ref[:, c0:c1])
        acc = acc + _dot(u.astype(BF16), w2_ref[c0:c1, :])
    h2 = _rms(acc, np_ref[...]).astype(BF16)
    gate = _sigmoid(_dot(h2, wpg_ref[...]))
    y = acc + gate * _dot(pe_ref[...].astype(BF16), wpe_ref[...])
    if final:
        y = _rms(y, fn_ref[...])
    o_ref[...] = y


def _ffn_ple(x, pe, lw, final_norm, layer, tm):
    n = x.shape[0]
    d_ff = lw["w_f1"].shape[1]
    half = d_ff // 2
    ff_chunks = tuple((c0, min(c0 + 256, d_ff)) for c0 in range(0, d_ff, 256))
    final = final_norm is not None
    row = pl.BlockSpec((tm, D_MODEL), lambda i: (i, 0))
    in_specs = [row, pl.BlockSpec((None, tm, PLE_DIM), lambda i: (layer, i, 0)), _const_spec((1, D_MODEL)),
                _const_spec((D_MODEL, d_ff)), _const_spec((D_MODEL, d_ff)), _const_spec((d_ff, D_MODEL)),
                _const_spec((1, D_MODEL)), _const_spec((D_MODEL, D_MODEL)), _const_spec((PLE_DIM, D_MODEL))]
    args = [x, pe, lw["norm_ffn"], lw["w_f1"], lw["w_f3"], lw["w_f2"], lw["norm_ple"], lw["w_ple_gate"],
            lw["w_ple"]]
    if final:
        in_specs.append(_const_spec((1, D_MODEL)))
        args.append(final_norm)
    return pl.pallas_call(
        functools.partial(_ffn_ple_kernel, ff_chunks=ff_chunks, final=final), grid=(n // tm,),
        in_specs=in_specs, out_specs=row, out_shape=jax.ShapeDtypeStruct(x.shape, F32),
        compiler_params=_params(("parallel",)), name="ffn_ple",
    )(*args)


def _rope_tables(pos):
    half = ROPE_DIM // 2
    inv = ROPE_THETA ** (-jnp.arange(0, ROPE_DIM, 2, dtype=F32) / ROPE_DIM)
    ang = pos.astype(F32)[:, None] * inv[None, :]
    cos, sin = jnp.cos(ang), jnp.sin(ang)
    n = pos.shape[0]
    pad = jnp.zeros((n, A_DH - ROPE_DIM), F32)
    cos64 = jnp.concatenate([cos, cos, pad + 1.0], axis=1)
    s1_64 = jnp.concatenate([jnp.zeros((n, half), F32), sin, pad], axis=1)
    s2_64 = jnp.concatenate([-sin, jnp.zeros((n, half), F32), pad], axis=1)
    return tuple(jnp.concatenate([t, t], axis=1) for t in (cos64, s1_64, s2_64))


def _prep_weights(w_in, norm_mix, w_ga2, b_ga, w_pa, w_pb, w_o, norm_ffn, w_f1, w_f3, w_f2, norm_ple, w_ple,
                  w_ple_gate):
    depth = w_in.shape[0]
    w_main = jnp.concatenate([w_in[:, :, :COL_AG], w_in[:, :, COL_AG + G_RANK:]], axis=2).astype(BF16)
    w_ag = jnp.pad(w_in[:, :, COL_AG:COL_AG + G_RANK], ((0, 0), (0, 0), (0, LANES - G_RANK))).astype(BF16)
    w_ga = jnp.pad(w_ga2, ((0, 0), (0, LANES - G_RANK), (0, 0))).astype(BF16)
    layers = []
    for l in range(depth):
        layers.append(dict(
            depth=depth,
            norm_mix=norm_mix[l][None], w_main=w_main[l], w_ag=w_ag[l], w_ga=w_ga[l], b_ga=b_ga[l][None],
            w_pa=w_pa[l].astype(BF16), w_pb=w_pb[l].astype(BF16), w_o=w_o[l].astype(BF16),
            norm_ffn=norm_ffn[l][None], w_f1=w_f1[l].astype(BF16), w_f3=w_f3[l].astype(BF16),
            w_f2=w_f2[l].astype(BF16), norm_ple=norm_ple[l][None], w_ple=w_ple[l].astype(BF16),
            w_ple_gate=w_ple_gate[l].astype(BF16)))
    return layers


def kernel(x_prompt, x_sample, p_prompt, p_sample, cache_k, cache_v, state_gla, w_in, norm_mix, lambda_q1,
           lambda_k1, lambda_q2, lambda_k2, a_subln, w_pa, w_ga2, b_ga, g_norm, w_pb, w_o, norm_ffn, w_f1, w_f3,
           w_f2, norm_ple, w_ple, w_ple_gate, final_norm):
    depth = w_in.shape[0]
    batch, seq, _ = x_prompt.shape
    dbatch, ts, _ = x_sample.shape
    past = cache_k.shape[2]
    n_p, n_s = batch * seq, dbatch * ts
    assert ts == CHUNK and seq % ATTN_TILE == 0 and seq % GLA_BLOCK == 0 and n_p % TOKEN_TILE == 0 and n_s % 8 == 0

    layers = _prep_weights(w_in, norm_mix, w_ga2, b_ga, w_pa, w_pb, w_o, norm_ffn, w_f1, w_f3, w_f2, norm_ple,
                           w_ple, w_ple_gate)
    rope_p = _rope_tables(jnp.arange(seq))
    rope_s = _rope_tables(jnp.tile(past + jnp.arange(ts), dbatch))
    lamp = jnp.pad(jnp.stack([lambda_q1, lambda_k1, lambda_q2, lambda_k2], axis=1),
                   ((0, 0), (0, 4), (0, LANES - A_DH)))
    fnorm = final_norm[None]
    tm_p = TOKEN_TILE
    tm_s = min(TOKEN_TILE, n_s)
    pe_p = p_prompt.reshape(depth, n_p, PLE_DIM)
    pe_s = p_sample.reshape(depth, n_s, PLE_DIM)
    cache_k4 = cache_k.reshape(depth, dbatch, past, A_HEADS * A_DV)
    cache_v4 = cache_v.reshape(depth, dbatch, past, A_HEADS * A_DV)

    xp = x_prompt.reshape(n_p, D_MODEL)
    xs = x_sample.reshape(n_s, D_MODEL)
    kp = vp = sp = ks = vs = ss = None
    for l in range(depth):
        lw = layers[l]
        last = fnorm if l == depth - 1 else None
        sub = a_subln[l][None]
        gn = g_norm[l][None]

        q, kp, vp, qkg, vg, rg, g2, kT, gTh, gTl, sga, sgb = _in_proj(xp, lw, rope_p, l, kp, vp, tm_p)
        oa = _attn_seq(q, kp, vp, lamp[l], a_subln[l][:, None], l, batch, seq, ATTN_TILE, ATTN_HEADS_PER_STEP)
        og, sp = _gla(qkg, vg, rg, g2, kT, gTh, gTl, gn, None, sp, l, depth, batch, seq, GLA_CHUNK, GLA_BLOCK)
        xp = _merge(xp, oa, og, sga, sgb, lw, tm_p)
        xp = _ffn_ple(xp, pe_p, lw, last, l, tm_p)

        q, ks, vs, qkg, vg, rg, g2, kT, gTh, gTl, sga, sgb = _in_proj(xs, lw, rope_s, l, ks, vs, tm_s)
        oa = _attn_sample(q, cache_k4, cache_v4, ks, vs, lamp[l], sub, l, dbatch, ts)
        by_batch = lambda a: a.reshape(G_KW, dbatch, ts).transpose(1, 0, 2)
        og, ss = _gla(qkg, vg, rg, g2, by_batch(kT), by_batch(gTh), by_batch(gTl), gn, state_gla, ss, l, depth,
                      dbatch, ts, ts, ts)
        xs = _merge(xs, oa, og, sga, sgb, lw, tm_s)
        xs = _ffn_ple(xs, pe_s, lw, last, l, tm_s)

    shape_kv = lambda a, b, t: a.reshape(depth, b, t, A_HEADS, A_DV)
    return (xp.reshape(batch, seq, D_MODEL), xs.reshape(dbatch, ts, D_MODEL),
            shape_kv(kp, batch, seq), shape_kv(vp, batch, seq), sp,
            shape_kv(ks, dbatch, ts), shape_kv(vs, dbatch, ts), ss)
```
